```python
import math
import jax, jax.numpy as jnp
from jax import lax
import numpy as np

D_MODEL = 1024
BATCH = 2
SEQ = 8192
DEPTH = 1

D_MIX = D_MODEL
CONV_CH = D_MIX // 2
CONV_GROUPS = 8
N_HEADS = 8
HEAD_DIM = 64
ATT_CH = N_HEADS * HEAD_DIM
CONV_WIDTH = 31
GRID_W = 64
MAX_KH = 8
KW = 16
EPS = 1e-6
PROJ_OUT = 3 * CONV_CH + 4 * ATT_CH

kernel_name = "hybrid_conformer_conv_natten_block"


def rms_norm(x, g):
    x32 = x.astype(jnp.float32)
    y = x32 * lax.rsqrt(jnp.mean(x32 * x32, axis=-1, keepdims=True) + EPS)
    return (y * g.astype(jnp.float32)).astype(x.dtype)


def layer_norm(x, g, b):
    x32 = x.astype(jnp.float32)
    mu = jnp.mean(x32, axis=-1, keepdims=True)
    var = jnp.mean(jnp.square(x32 - mu), axis=-1, keepdims=True)
    y = (x32 - mu) * lax.rsqrt(var + EPS)
    return (y * g.astype(jnp.float32) + b.astype(jnp.float32)).astype(x.dtype)


def grouped_rms_norm(y, g, n_groups):
    B, T, C = y.shape
    y32 = y.astype(jnp.float32).reshape(B, T, n_groups, C // n_groups)
    y32 = y32 * lax.rsqrt(jnp.mean(y32 * y32, axis=-1, keepdims=True) + EPS)
    return (y32.reshape(B, T, C) * g.astype(jnp.float32)).astype(y.dtype)


def conformer_conv(glu_a, glu_b, dw_w, dw_b, cln_g, cln_b, pw_w, pw_b):
    u = glu_a * jax.nn.sigmoid(glu_b)
    pad = CONV_WIDTH // 2
    u = lax.conv_general_dilated(
        u, dw_w[:, None, :].astype(u.dtype), window_strides=(1,),
        padding=[(pad, pad)], dimension_numbers=("NWC", "WIO", "NWC"),
        feature_group_count=u.shape[-1]) + dw_b
    u = layer_norm(u, cln_g, cln_b)
    u = jax.nn.silu(u)
    return u @ pw_w + pw_b


def neighbourhood_attention(q, k, v, rpb):
    B, T, _ = q.shape
    rows = T // GRID_W
    kh = min(MAX_KH, rows)

    def to_grid(t):
        return t.reshape(B, rows, GRID_W, N_HEADS, HEAD_DIM).transpose(0, 3, 1, 2, 4)

    qg, kg, vg = to_grid(q), to_grid(k), to_grid(v)
    scale = HEAD_DIM ** -0.5

    row_ids = jnp.arange(rows)
    row_start = jnp.clip(row_ids - kh // 2, 0, rows - kh)
    col_ids = jnp.arange(GRID_W)
    col_start = jnp.clip(col_ids - KW // 2, 0, GRID_W - KW)
    col_idx = col_start[:, None] + jnp.arange(KW)[None, :]
    col_rel = col_idx - col_ids[:, None] + (KW - 1)

    def one_row(args):
        r, rs = args
        q_row = lax.dynamic_index_in_dim(qg, r, axis=2, keepdims=False)
        k_rows = lax.dynamic_slice_in_dim(kg, rs, kh, axis=2)
        v_rows = lax.dynamic_slice_in_dim(vg, rs, kh, axis=2)
        k_win = k_rows[:, :, :, col_idx, :]
        v_win = v_rows[:, :, :, col_idx, :]
        s = jnp.einsum("bhwd,bhiwjd->bhwij", q_row, k_win).astype(jnp.float32) * scale
        row_rel = rs + jnp.arange(kh) - r + (MAX_KH - 1)
        bias = rpb[:, row_rel[:, None, None], col_rel[None, :, :]]
        s = s + bias.transpose(0, 2, 1, 3).astype(jnp.float32)[None]
        p = jax.nn.softmax(s.reshape(B, N_HEADS, GRID_W, kh * KW), axis=-1)
        p = p.reshape(B, N_HEADS, GRID_W, kh, KW).astype(v.dtype)
        return jnp.einsum("bhwij,bhiwjd->bhwd", p, v_win)

    out = lax.map(one_row, (row_ids, row_start))
    return out.transpose(1, 0, 3, 2, 4).reshape(B, T, ATT_CH)


def setup_inputs(seed: int = 0) -> dict:
    key = jax.random.key(seed)
    ks = jax.random.split(key, 16)
    f32 = jnp.float32
    nrm = lambda k, s: jax.random.normal(k, s, f32)
    return {
        "x": nrm(ks[0], (BATCH, SEQ, D_MODEL)),
        "ln_g": 1.0 + 0.01 * nrm(ks[1], (DEPTH, D_MODEL)),
        "w_in": nrm(ks[2], (DEPTH, D_MODEL, PROJ_OUT)) * D_MODEL ** -0.5,
        "b_in": 0.01 * nrm(ks[3], (DEPTH, PROJ_OUT)),
        "dw_w": nrm(ks[4], (DEPTH, CONV_WIDTH, CONV_CH)) * CONV_WIDTH ** -0.5,
        "dw_b": 0.01 * nrm(ks[5], (DEPTH, CONV_CH)),
        "cln_g": 1.0 + 0.01 * nrm(ks[6], (DEPTH, CONV_CH)),
        "cln_b": 0.01 * nrm(ks[7], (DEPTH, CONV_CH)),
        "pw_w": nrm(ks[8], (DEPTH, CONV_CH, CONV_CH)) * CONV_CH ** -0.5,
        "pw_b": 0.01 * nrm(ks[9], (DEPTH, CONV_CH)),
        "rpb": 0.02 * nrm(ks[10], (DEPTH, N_HEADS, 2 * MAX_KH - 1, 2 * KW - 1)),
        "gn_conv_g": 1.0 + 0.01 * nrm(ks[11], (DEPTH, CONV_CH)),
        "gn_att_g": 1.0 + 0.01 * nrm(ks[12], (DEPTH, ATT_CH)),
        "w_out": nrm(ks[13], (DEPTH, D_MIX, D_MODEL)) * D_MIX ** -0.5,
        "final_g": 1.0 + 0.01 * nrm(ks[14], (D_MODEL,)),
    }


def reference(x, ln_g, w_in, b_in, dw_w, dw_b, cln_g, cln_b, pw_w, pw_b, rpb,
              gn_conv_g, gn_att_g, w_out, final_g):
    h = x
    cuts = [CONV_CH, 2 * CONV_CH, 3 * CONV_CH, 3 * CONV_CH + ATT_CH,
            3 * CONV_CH + 2 * ATT_CH, 3 * CONV_CH + 3 * ATT_CH]
    for l in range(DEPTH):
        hn = rms_norm(h, ln_g[l])
        proj = hn @ w_in[l] + b_in[l]
        glu_a, glu_b, z_conv, q, k, v, z_att = jnp.split(proj, cuts, axis=-1)
        y_conv = conformer_conv(glu_a, glu_b, dw_w[l], dw_b[l], cln_g[l], cln_b[l],
                                pw_w[l], pw_b[l])
        y_att = neighbourhood_attention(q, k, v, rpb[l])
        y_conv = grouped_rms_norm(y_conv, gn_conv_g[l], CONV_GROUPS) * jax.nn.silu(z_conv)
        y_att = grouped_rms_norm(y_att, gn_att_g[l], N_HEADS) * jax.nn.silu(z_att)
        y = jnp.concatenate([y_conv, y_att], axis=-1)
        h = h + y @ w_out[l]
    return rms_norm(h, final_g)
```

```python
import functools

import jax
import jax.numpy as jnp
from jax import lax
from jax.experimental import pallas as pl
from jax.experimental.pallas import tpu as pltpu

D_MODEL = 1024
CONV_CH = 512
CONV_GROUPS = 8
N_HEADS = 8
HEAD_DIM = 64
ATT_CH = N_HEADS * HEAD_DIM
CONV_WIDTH = 31
CONV_PAD = CONV_WIDTH // 2
GRID_W = 64
MAX_KH = 8
KW = 16
EPS = 1e-6
PROJ_OUT = 3 * CONV_CH + 4 * ATT_CH
N_PROJ_GROUPS = PROJ_OUT // CONV_CH

HALO = 16
TOKEN_TILE = 512
ROWS_PER_TILE = TOKEN_TILE // GRID_W
KV_HALO_ROWS = MAX_KH // 2
KV_HALO = KV_HALO_ROWS * GRID_W
HEADS_PER_GROUP = 4
GROUP_CH = HEADS_PER_GROUP * HEAD_DIM
N_HEAD_GROUPS = N_HEADS // HEADS_PER_GROUP
WIN_KEYS = MAX_KH * GRID_W
MASK_VALUE = -1e30
VMEM_LIMIT_BYTES = 56 * 1024 * 1024


def _silu(z):
    return z * jax.nn.sigmoid(z)


def _group_mean_square(y, n_groups):
    c = y.shape[-1]
    gsz = c // n_groups
    gi = lax.broadcasted_iota(jnp.int32, (c, c), 0) // gsz
    gj = lax.broadcasted_iota(jnp.int32, (c, c), 1) // gsz
    same = (gi == gj).astype(jnp.bfloat16)
    ysq = y * y
    hi = ysq.astype(jnp.bfloat16)
    lo = (ysq - hi.astype(jnp.float32)).astype(jnp.bfloat16)
    ss = (jnp.dot(hi, same, preferred_element_type=jnp.float32)
          + jnp.dot(lo, same, preferred_element_type=jnp.float32))
    return ss * (1.0 / gsz)


def _in_proj_kernel(x_ref, g_ref, w_ref, b_ref, u_ref, gc_ref, q_ref, k_ref, v_ref, ga_ref):
    x = x_ref[...]
    ms = jnp.mean(x * x, axis=-1, keepdims=True)
    hn = (x * lax.rsqrt(ms + EPS) * g_ref[...]).astype(jnp.bfloat16)

    def proj(g):
        cols = slice(g * CONV_CH, (g + 1) * CONV_CH)
        return jnp.dot(hn, w_ref[:, cols], preferred_element_type=jnp.float32) + b_ref[:, cols]

    u_ref[...] = (proj(0) * jax.nn.sigmoid(proj(1))).astype(u_ref.dtype)
    gc_ref[...] = _silu(proj(2)).astype(gc_ref.dtype)
    q_ref[...] = (proj(3) * (HEAD_DIM ** -0.5)).astype(q_ref.dtype)
    k_ref[...] = proj(4).astype(k_ref.dtype)
    v_ref[...] = proj(5).astype(v_ref.dtype)
    ga_ref[...] = _silu(proj(6)).astype(ga_ref.dtype)


def _in_proj(x2, ln_g, w_in, b_in):
    n_tok = x2.shape[0]
    out = jax.ShapeDtypeStruct((n_tok, CONV_CH), jnp.bfloat16)
    tile = pl.BlockSpec((TOKEN_TILE, CONV_CH), lambda i: (i, 0))
    return pl.pallas_call(
        _in_proj_kernel,
        grid=(n_tok // TOKEN_TILE,),
        in_specs=[
            pl.BlockSpec((TOKEN_TILE, D_MODEL), lambda i: (i, 0)),
            pl.BlockSpec((1, D_MODEL), lambda i: (0, 0)),
            pl.BlockSpec((D_MODEL, PROJ_OUT), lambda i: (0, 0)),
            pl.BlockSpec((1, PROJ_OUT), lambda i: (0, 0)),
        ],
        out_specs=[tile] * 6,
        out_shape=[out] * 6,
        compiler_params=pltpu.CompilerParams(
            dimension_semantics=("arbitrary",), vmem_limit_bytes=VMEM_LIMIT_BYTES),
        name="in_proj",
    )(x2, ln_g, w_in, b_in)


def _conv_mix_kernel(up_ref, um_ref, un_ref, gc_ref, dww_ref, dwb_ref, lng_ref, lnb_ref,
                     pww_ref, pwb_ref, gng_ref, yc_ref, buf_ref):
    t = pl.program_id(1)
    n_t = pl.num_programs(1)
    prev = up_ref[...].astype(jnp.float32)
    nxt = un_ref[...].astype(jnp.float32)
    buf_ref[0:HALO, :] = jnp.where(t > 0, prev, 0.0)
    buf_ref[HALO:HALO + TOKEN_TILE, :] = um_ref[...].astype(jnp.float32)
    buf_ref[HALO + TOKEN_TILE:, :] = jnp.where(t < n_t - 1, nxt, 0.0)

    row_chunk = 128
    lane_chunk = 128
    pieces = []
    for r0 in range(0, TOKEN_TILE, row_chunk):
        row_pieces = []
        for c0 in range(0, CONV_CH, lane_chunk):
            acc = jnp.zeros((row_chunk, lane_chunk), jnp.float32)
            for j in range(CONV_WIDTH):
                start = HALO - CONV_PAD + j + r0
                acc = acc + (buf_ref[start:start + row_chunk, c0:c0 + lane_chunk]
                             * dww_ref[j:j + 1, c0:c0 + lane_chunk])
            row_pieces.append(acc)
        pieces.append(jnp.concatenate(row_pieces, axis=-1))
    c = jnp.concatenate(pieces, axis=0) + dwb_ref[...]

    mu = jnp.mean(c, axis=-1, keepdims=True)
    cc = c - mu
    var = jnp.mean(cc * cc, axis=-1, keepdims=True)
    n = cc * lax.rsqrt(var + EPS) * lng_ref[...] + lnb_ref[...]
    s = _silu(n).astype(jnp.bfloat16)
    y = jnp.dot(s, pww_ref[...], preferred_element_type=jnp.float32) + pwb_ref[...]
    ms = _group_mean_square(y, CONV_GROUPS)
    yn = y * lax.rsqrt(ms + EPS) * gng_ref[...]
    yc_ref[...] = (yn * gc_ref[...].astype(jnp.float32)).astype(yc_ref.dtype)


def _conv_mix(u, gconv, dw_w, dw_b, cln_g, cln_b, pw_w, pw_b, gn_g, batch, seq):
    tiles = seq // TOKEN_TILE
    halo_per_tile = TOKEN_TILE // HALO
    n_halo_blocks = batch * seq // HALO

    def main_map(b, t):
        return (b * tiles + t, 0)

    def prev_map(b, t):
        return (jnp.maximum((b * tiles + t) * halo_per_tile - 1, 0), 0)

    def next_map(b, t):
        return (jnp.minimum((b * tiles + t + 1) * halo_per_tile, n_halo_blocks - 1), 0)

    const = lambda b, t: (0, 0)
    return pl.pallas_call(
        _conv_mix_kernel,
        grid=(batch, tiles),
        in_specs=[
            pl.BlockSpec((HALO, CONV_CH), prev_map),
            pl.BlockSpec((TOKEN_TILE, CONV_CH), main_map),
            pl.BlockSpec((HALO, CONV_CH), next_map),
            pl.BlockSpec((TOKEN_TILE, CONV_CH), main_map),
            pl.BlockSpec((CONV_WIDTH, CONV_CH), const),
            pl.BlockSpec((1, CONV_CH), const),
            pl.BlockSpec((1, CONV_CH), const),
            pl.BlockSpec((1, CONV_CH), const),
            pl.BlockSpec((CONV_CH, CONV_CH), const),
            pl.BlockSpec((1, CONV_CH), const),
            pl.BlockSpec((1, CONV_CH), const),
        ],
        out_specs=pl.BlockSpec((TOKEN_TILE, CONV_CH), main_map),
        out_shape=jax.ShapeDtypeStruct((batch * seq, CONV_CH), jnp.bfloat16),
        scratch_shapes=[pltpu.VMEM((TOKEN_TILE + 2 * HALO, CONV_CH), jnp.float32)],
        compiler_params=pltpu.CompilerParams(
            dimension_semantics=("arbitrary", "arbitrary"), vmem_limit_bytes=VMEM_LIMIT_BYTES),
        name="conv_mix",
    )(u, u, u, gconv, dw_w, dw_b, cln_g, cln_b, pw_w, pw_b, gn_g)


def _attn_out_kernel(q_ref, kp_ref, km_ref, kn_ref, vp_ref, vm_ref, vn_ref, ga_ref, yc_ref,
                     x_ref, bias_ref, gng_ref, wo_ref, fg_ref, o_ref,
                     kbuf_ref, vbuf_ref, ya_ref, *, n_rows):
    t = pl.program_id(1)
    kbuf_ref[0:KV_HALO, :] = kp_ref[...]
    kbuf_ref[KV_HALO:KV_HALO + TOKEN_TILE, :] = km_ref[...]
    kbuf_ref[KV_HALO + TOKEN_TILE:, :] = kn_ref[...]
    vbuf_ref[0:KV_HALO, :] = vp_ref[...]
    vbuf_ref[KV_HALO:KV_HALO + TOKEN_TILE, :] = vm_ref[...]
    vbuf_ref[KV_HALO + TOKEN_TILE:, :] = vn_ref[...]

    lane_head = lax.broadcasted_iota(jnp.int32, (GRID_W, GROUP_CH), 1) // HEAD_DIM
    row0 = t * ROWS_PER_TILE
    for i in range(ROWS_PER_TILE):
        r = row0 + i
        rs = jnp.clip(r - MAX_KH // 2, 0, n_rows - MAX_KH)
        off = pl.multiple_of((rs - row0 + KV_HALO_ROWS) * GRID_W, GRID_W)
        cls = r - rs
        q_row = q_ref[i * GRID_W:(i + 1) * GRID_W, :]
        outs = []
        for gi in range(N_HEAD_GROUPS):
            ch = slice(gi * GROUP_CH, (gi + 1) * GROUP_CH)
            qg = q_row[:, ch]
            qbd = jnp.concatenate(
                [jnp.where(lane_head == h, qg, jnp.zeros_like(qg)) for h in range(HEADS_PER_GROUP)],
                axis=0)
            k_win = kbuf_ref[pl.ds(off, WIN_KEYS), ch]
            v_win = vbuf_ref[pl.ds(off, WIN_KEYS), ch]
            s = lax.dot_general(qbd, k_win, (((1,), (1,)), ((), ())),
                                preferred_element_type=jnp.float32)
            s = s + bias_ref[cls, gi]
            m = jnp.max(s, axis=-1, keepdims=True)
            e = jnp.exp(s - m)
            l = jnp.sum(e, axis=-1, keepdims=True)
            pv = jnp.dot(e.astype(jnp.bfloat16), v_win, preferred_element_type=jnp.float32)
            pv = pv / l
            o = jnp.zeros((GRID_W, GROUP_CH), jnp.float32)
            for h in range(HEADS_PER_GROUP):
                o = o + jnp.where(lane_head == h, pv[h * GRID_W:(h + 1) * GRID_W, :], 0.0)
            outs.append(o)
        ya_ref[i * GRID_W:(i + 1) * GRID_W, :] = jnp.concatenate(outs, axis=-1)

    ya = ya_ref[...]
    ms = _group_mean_square(ya, N_HEADS)
    yan = ya * lax.rsqrt(ms + EPS) * gng_ref[...] * ga_ref[...].astype(jnp.float32)
    h = (x_ref[...]
         + jnp.dot(yc_ref[...], wo_ref[0:CONV_CH, :], preferred_element_type=jnp.float32)
         + jnp.dot(yan.astype(jnp.bfloat16), wo_ref[CONV_CH:, :],
                   preferred_element_type=jnp.float32))
    hms = jnp.mean(h * h, axis=-1, keepdims=True)
    o_ref[...] = (h * lax.rsqrt(hms + EPS) * fg_ref[...]).astype(o_ref.dtype)


def _attn_out(q, k, v, gatt, yc, x2, bias_tab, gn_g, w_out, final_g, batch, seq):
    tiles = seq // TOKEN_TILE
    n_rows = seq // GRID_W
    halo_per_tile = TOKEN_TILE // KV_HALO
    n_halo_blocks = batch * seq // KV_HALO

    def main_map(b, t):
        return (b * tiles + t, 0)

    def prev_map(b, t):
        return (jnp.maximum((b * tiles + t) * halo_per_tile - 1, 0), 0)

    def next_map(b, t):
        return (jnp.minimum((b * tiles + t + 1) * halo_per_tile, n_halo_blocks - 1), 0)

    const2 = lambda b, t: (0, 0)
    tile = pl.BlockSpec((TOKEN_TILE, ATT_CH), main_map)
    halo_p = pl.BlockSpec((KV_HALO, ATT_CH), prev_map)
    halo_n = pl.BlockSpec((KV_HALO, ATT_CH), next_map)
    return pl.pallas_call(
        functools.partial(_attn_out_kernel, n_rows=n_rows),
        grid=(batch, tiles),
        in_specs=[
            tile, halo_p, tile, halo_n, halo_p, tile, halo_n, tile, tile,
            pl.BlockSpec((TOKEN_TILE, D_MODEL), main_map),
            pl.BlockSpec(bias_tab.shape, lambda b, t: (0, 0, 0, 0)),
            pl.BlockSpec((1, ATT_CH), const2),
            pl.BlockSpec((D_MODEL, D_MODEL), const2),
            pl.BlockSpec((1, D_MODEL), const2),
        ],
        out_specs=pl.BlockSpec((TOKEN_TILE, D_MODEL), main_map),
        out_shape=jax.ShapeDtypeStruct((batch * seq, D_MODEL), x2.dtype),
        scratch_shapes=[
            pltpu.VMEM((TOKEN_TILE + 2 * KV_HALO, ATT_CH), jnp.bfloat16),
            pltpu.VMEM((TOKEN_TILE + 2 * KV_HALO, ATT_CH), jnp.bfloat16),
            pltpu.VMEM((TOKEN_TILE, ATT_CH), jnp.float32),
        ],
        compiler_params=pltpu.CompilerParams(
            dimension_semantics=("arbitrary", "arbitrary"), vmem_limit_bytes=VMEM_LIMIT_BYTES),
        name="attn_out",
    )(q, k, k, k, v, v, v, gatt, yc, x2, bias_tab, gn_g, w_out, final_g)


def _dense_bias_table(rpb):
    w = jnp.arange(GRID_W)
    c = jnp.arange(GRID_W)
    col_start = jnp.clip(w - KW // 2, 0, GRID_W - KW)
    valid = (c[None, :] >= col_start[:, None]) & (c[None, :] < col_start[:, None] + KW)
    col_rel = jnp.clip(c[None, :] - w[:, None] + (KW - 1), 0, 2 * KW - 2)
    cls = jnp.arange(MAX_KH)
    i = jnp.arange(MAX_KH)
    row_rel = i[None, :] - cls[:, None] + (MAX_KH - 1)
    tab = rpb[:, row_rel[:, :, None, None], col_rel[None, None, :, :]]
    tab = jnp.where(valid[None, None, None], tab, MASK_VALUE)
    tab = tab.transpose(1, 0, 3, 2, 4)
    return tab.reshape(MAX_KH, N_HEAD_GROUPS, HEADS_PER_GROUP * GRID_W, WIN_KEYS).astype(jnp.float32)


def kernel(x, ln_g, w_in, b_in, dw_w, dw_b, cln_g, cln_b, pw_w, pw_b, rpb,
           gn_conv_g, gn_att_g, w_out, final_g):
    batch, seq, d_model = x.shape
    depth = w_in.shape[0]
    assert d_model == D_MODEL and seq % TOKEN_TILE == 0 and seq // GRID_W >= MAX_KH
    bf16 = jnp.bfloat16
    h = x.reshape(batch * seq, d_model)
    for l in range(depth):
        last = l == depth - 1
        u, gconv, q, k, v, gatt = _in_proj(h, ln_g[l][None], w_in[l].astype(bf16), b_in[l][None])
        yc = _conv_mix(u, gconv, dw_w[l], dw_b[l][None], cln_g[l][None], cln_b[l][None],
                       pw_w[l].astype(bf16), pw_b[l][None], gn_conv_g[l][None], batch, seq)
        assert last, "multi-layer stacks need an un-normalised attn_out variant"
        h = _attn_out(q, k, v, gatt, yc, h, _dense_bias_table(rpb[l]), gn_att_g[l][None],
                      w_out[l].astype(bf16), final_g[None], batch, seq)
    return h.reshape(batch, seq, d_model)
```

```python
import functools

import jax
import jax.numpy as jnp
from jax import lax
from jax.experimental import pallas as pl
from jax.experimental.pallas import tpu as pltpu

D_MODEL = 1024
CONV_CH = 512
CONV_GROUPS = 8
N_HEADS = 8
HEAD_DIM = 64
ATT_CH = N_HEADS * HEAD_DIM
CONV_WIDTH = 31
CONV_PAD = CONV_WIDTH // 2
GRID_W = 64
MAX_KH = 8
KW = 16
EPS = 1e-6
PROJ_OUT = 3 * CONV_CH + 4 * ATT_CH
N_PROJ_GROUPS = PROJ_OUT // CONV_CH

HALO = 16
TOKEN_TILE = 512
ROWS_PER_TILE = TOKEN_TILE // GRID_W
KV_HALO_ROWS = MAX_KH // 2
KV_HALO = KV_HALO_ROWS * GRID_W
HEADS_PER_GROUP = 4
GROUP_CH = HEADS_PER_GROUP * HEAD_DIM
N_HEAD_GROUPS = N_HEADS // HEADS_PER_GROUP
WIN_KEYS = MAX_KH * GRID_W
MASK_VALUE = -1e30
VMEM_LIMIT_BYTES = 56 * 1024 * 1024


def _silu(z):
    return z * jax.nn.sigmoid(z)


def _group_mean_square(y, n_groups):
    c = y.shape[-1]
    gsz = c // n_groups
    gi = lax.broadcasted_iota(jnp.int32, (c, c), 0) // gsz
    gj = lax.broadcasted_iota(jnp.int32, (c, c), 1) // gsz
    same = (gi == gj).astype(jnp.bfloat16)
    ysq = y * y
    hi = ysq.astype(jnp.bfloat16)
    lo = (ysq - hi.astype(jnp.float32)).astype(jnp.bfloat16)
    ss = (jnp.dot(hi, same, preferred_element_type=jnp.float32)
          + jnp.dot(lo, same, preferred_element_type=jnp.float32))
    return ss * (1.0 / gsz)


def _in_proj_kernel(x_ref, g_ref, w_ref, b_ref, u_ref, gc_ref, q_ref, k_ref, v_ref, ga_ref):
    x = x_ref[...]
    ms = jnp.mean(x * x, axis=-1, keepdims=True)
    hn = (x * lax.rsqrt(ms + EPS) * g_ref[...]).astype(jnp.bfloat16)

    def proj(g):
        cols = slice(g * CONV_CH, (g + 1) * CONV_CH)
        return jnp.dot(hn, w_ref[:, cols], preferred_element_type=jnp.float32) + b_ref[:, cols]

    u_ref[...] = (proj(0) * jax.nn.sigmoid(proj(1))).astype(u_ref.dtype)
    gc_ref[...] = _silu(proj(2)).astype(gc_ref.dtype)
    q_ref[...] = (proj(3) * (HEAD_DIM ** -0.5)).astype(q_ref.dtype)
    k_ref[...] = proj(4).astype(k_ref.dtype)
    v_ref[...] = proj(5).astype(v_ref.dtype)
    ga_ref[...] = _silu(proj(6)).astype(ga_ref.dtype)


def _in_proj(x2, ln_g, w_in, b_in):
    n_tok = x2.shape[0]
    out = jax.ShapeDtypeStruct((n_tok, CONV_CH), jnp.bfloat16)
    tile = pl.BlockSpec((TOKEN_TILE, CONV_CH), lambda i: (i, 0))
    return pl.pallas_call(
        _in_proj_kernel,
        grid=(n_tok // TOKEN_TILE,),
        in_specs=[
            pl.BlockSpec((TOKEN_TILE, D_MODEL), lambda i: (i, 0)),
            pl.BlockSpec((1, D_MODEL), lambda i: (0, 0)),
            pl.BlockSpec((D_MODEL, PROJ_OUT), lambda i: (0, 0)),
            pl.BlockSpec((1, PROJ_OUT), lambda i: (0, 0)),
        ],
        out_specs=[tile] * 6,
        out_shape=[out] * 6,
        compiler_params=pltpu.CompilerParams(
            dimension_semantics=("arbitrary",), vmem_limit_bytes=VMEM_LIMIT_BYTES),
        name="in_proj",
    )(x2, ln_g, w_in, b_in)


def _conv_mix_kernel(up_ref, um_ref, un_ref, gc_ref, dww_ref, dwb_ref, lng_ref, lnb_ref,
                     pww_ref, pwb_ref, gng_ref, yc_ref, buf_ref):
    t = pl.program_id(1)
    n_t = pl.num_programs(1)
    prev = up_ref[...].astype(jnp.float32)
    nxt = un_ref[...].astype(jnp.float32)
    buf_ref[0:HALO, :] = jnp.where(t > 0, prev, 0.0)
    buf_ref[HALO:HALO + TOKEN_TILE, :] = um_ref[...].astype(jnp.float32)
    buf_ref[HALO + TOKEN_TILE:, :] = jnp.where(t < n_t - 1, nxt, 0.0)

    row_chunk = 128
    lane_chunk = 128
    pieces = []
    for r0 in range(0, TOKEN_TILE, row_chunk):
        row_pieces = []
        for c0 in range(0, CONV_CH, lane_chunk):
            acc = jnp.zeros((row_chunk, lane_chunk), jnp.float32)
            for j in range(CONV_WIDTH):
                start = HALO - CONV_PAD + j + r0
                acc = acc + (buf_ref[start:start + row_chunk, c0:c0 + lane_chunk]
                             * dww_ref[j:j + 1, c0:c0 + lane_chunk])
            row_pieces.append(acc)
        pieces.append(jnp.concatenate(row_pieces, axis=-1))
    c = jnp.concatenate(pieces, axis=0) + dwb_ref[...]

    mu = jnp.mean(c, axis=-1, keepdims=True)
    cc = c - mu
    var = jnp.mean(cc * cc, axis=-1, keepdims=True)
    n = cc * lax.rsqrt(var + EPS) * lng_ref[...] + lnb_ref[...]
    s = _silu(n).astype(jnp.bfloat16)
    y = jnp.dot(s, pww_ref[...], preferred_element_type=jnp.float32) + pwb_ref[...]
    ms = _group_mean_square(y, CONV_GROUPS)
    yn = y * lax.rsqrt(ms + EPS) * gng_ref[...]
    yc_ref[...] = (yn * gc_ref[...].astype(jnp.float32)).astype(yc_ref.dtype)


def _conv_mix(u, gconv, dw_w, dw_b, cln_g, cln_b, pw_w, pw_b, gn_g, batch, seq):
    tiles = seq // TOKEN_TILE
    halo_per_tile = TOKEN_TILE // HALO
    n_halo_blocks = batch * seq // HALO

    def main_map(b, t):
        return (b * tiles + t, 0)

    def prev_map(b, t):
        return (jnp.maximum((b * tiles + t) * halo_per_tile - 1, 0), 0)

    def next_map(b, t):
        return (jnp.minimum((b * tiles + t + 1) * halo_per_tile, n_halo_blocks - 1), 0)

    const = lambda b, t: (0, 0)
    return pl.pallas_call(
        _conv_mix_kernel,
        grid=(batch, tiles),
        in_specs=[
            pl.BlockSpec((HALO, CONV_CH), prev_map),
            pl.BlockSpec((TOKEN_TILE, CONV_CH), main_map),
            pl.BlockSpec((HALO, CONV_CH), next_map),
            pl.BlockSpec((TOKEN_TILE, CONV_CH), main_map),
            pl.BlockSpec((CONV_WIDTH, CONV_CH), const),
            pl.BlockSpec((1, CONV_CH), const),
            pl.BlockSpec((1, CONV_CH), const),
            pl.BlockSpec((1, CONV_CH), const),
            pl.BlockSpec((CONV_CH, CONV_CH), const),
            pl.BlockSpec((1, CONV_CH), const),
            pl.BlockSpec((1, CONV_CH), const),
        ],
        out_specs=pl.BlockSpec((TOKEN_TILE, CONV_CH), main_map),
        out_shape=jax.ShapeDtypeStruct((batch * seq, CONV_CH), jnp.bfloat16),
        scratch_shapes=[pltpu.VMEM((TOKEN_TILE + 2 * HALO, CONV_CH), jnp.float32)],
        compiler_params=pltpu.CompilerParams(
            dimension_semantics=("arbitrary", "arbitrary"), vmem_limit_bytes=VMEM_LIMIT_BYTES),
        name="conv_mix",
    )(u, u, u, gconv, dw_w, dw_b, cln_g, cln_b, pw_w, pw_b, gn_g)


def _attn_out_kernel(q_ref, kp_ref, km_ref, kn_ref, vp_ref, vm_ref, vn_ref, ga_ref, yc_ref,
                     x_ref, bias_ref, gng_ref, wo_ref, fg_ref, o_ref,
                     kbuf_ref, vbuf_ref, ya_ref, *, n_rows):
    t = pl.program_id(1)
    kbuf_ref[0:KV_HALO, :] = kp_ref[...]
    kbuf_ref[KV_HALO:KV_HALO + TOKEN_TILE, :] = km_ref[...]
    kbuf_ref[KV_HALO + TOKEN_TILE:, :] = kn_ref[...]
    vbuf_ref[0:KV_HALO, :] = vp_ref[...]
    vbuf_ref[KV_HALO:KV_HALO + TOKEN_TILE, :] = vm_ref[...]
    vbuf_ref[KV_HALO + TOKEN_TILE:, :] = vn_ref[...]

    lane_head = lax.broadcasted_iota(jnp.int32, (GRID_W, GROUP_CH), 1) // HEAD_DIM
    row0 = t * ROWS_PER_TILE
    for i in range(ROWS_PER_TILE):
        r = row0 + i
        rs = jnp.clip(r - MAX_KH // 2, 0, n_rows - MAX_KH)
        off = pl.multiple_of((rs - row0 + KV_HALO_ROWS) * GRID_W, GRID_W)
        cls = r - rs
        q_row = q_ref[i * GRID_W:(i + 1) * GRID_W, :]
        outs = []
        for gi in range(N_HEAD_GROUPS):
            ch = slice(gi * GROUP_CH, (gi + 1) * GROUP_CH)
            qg = q_row[:, ch]
            qbd = jnp.concatenate(
                [jnp.where(lane_head == h, qg, jnp.zeros_like(qg)) for h in range(HEADS_PER_GROUP)],
                axis=0)
            k_win = kbuf_ref[pl.ds(off, WIN_KEYS), ch]
            v_win = vbuf_ref[pl.ds(off, WIN_KEYS), ch]
            s = lax.dot_general(qbd, k_win, (((1,), (1,)), ((), ())),
                                preferred_element_type=jnp.float32)
            s = s + bias_ref[cls, gi]
            m = jnp.max(s, axis=-1, keepdims=True)
            e = jnp.exp(s - m)
            l = jnp.sum(e, axis=-1, keepdims=True)
            pv = jnp.dot(e.astype(jnp.bfloat16), v_win, preferred_element_type=jnp.float32)
            pv = pv / l
            o = jnp.zeros((GRID_W, GROUP_CH), jnp.float32)
            for h in range(HEADS_PER_GROUP):
                o = o + jnp.where(lane_head == h, pv[h * GRID_W:(h + 1) * GRID_W, :], 0.0)
            outs.append(o)
        ya_ref[i * GRID_W:(i + 1) * GRID_W, :] = jnp.concatenate(outs, axis=-1)

    ya = ya_ref[...]
    ms = _group_mean_square(ya, N_HEADS)
    yan = ya * lax.rsqrt(ms + EPS) * gng_ref[...] * ga_ref[...].astype(jnp.float32)
    h = (x_ref[...]
         + jnp.dot(yc_ref[...], wo_ref[0:CONV_CH, :], preferred_element_type=jnp.float32)
         + jnp.dot(yan.astype(jnp.bfloat16), wo_ref[CONV_CH:, :],
                   preferred_element_type=jnp.float32))
    hms = jnp.mean(h * h, axis=-1, keepdims=True)
    o_ref[...] = (h * lax.rsqrt(hms + EPS) * fg_ref[...]).astype(o_ref.dtype)


def _attn_out(q, k, v, gatt, yc, x2, bias_tab, gn_g, w_out, final_g, batch, seq):
    tiles = seq // TOKEN_TILE
    n_rows = seq // GRID_W
    halo_per_tile = TOKEN_TILE // KV_HALO
    n_halo_blocks = batch * seq // KV_HALO

    def main_map(b, t):
        return (b * tiles + t, 0)

    def prev_map(b, t):
        return (jnp.maximum((b * tiles + t) * halo_per_tile - 1, 0), 0)

    def next_map(b, t):
        return (jnp.minimum((b * tiles + t + 1) * halo_per_tile, n_halo_blocks - 1), 0)

    const2 = lambda b, t: (0, 0)
    tile = pl.BlockSpec((TOKEN_TILE, ATT_CH), main_map)
    halo_p = pl.BlockSpec((KV_HALO, ATT_CH), prev_map)
    halo_n = pl.BlockSpec((KV_HALO, ATT_CH), next_map)
    return pl.pallas_call(
        functools.partial(_attn_out_kernel, n_rows=n_rows),
        grid=(batch, tiles),
        in_specs=[
            tile, halo_p, tile, halo_n, halo_p, tile, halo_n, tile, tile,
            pl.BlockSpec((TOKEN_TILE, D_MODEL), main_map),
            pl.BlockSpec(bias_tab.shape, lambda b, t: (0, 0, 0, 0)),
            pl.BlockSpec((1, ATT_CH), const2),
            pl.BlockSpec((D_MODEL, D_MODEL), const2),
            pl.BlockSpec((1, D_MODEL), const2),
        ],
        out_specs=pl.BlockSpec((TOKEN_TILE, D_MODEL), main_map),
        out_shape=jax.ShapeDtypeStruct((batch * seq, D_MODEL), x2.dtype),
        scratch_shapes=[
            pltpu.VMEM((TOKEN_TILE + 2 * KV_HALO, ATT_CH), jnp.bfloat16),
            pltpu.VMEM((TOKEN_TILE + 2 * KV_HALO, ATT_CH), jnp.bfloat16),
            pltpu.VMEM((TOKEN_TILE, ATT_CH), jnp.float32),
        ],
        compiler_params=pltpu.CompilerParams(
            dimension_semantics=("arbitrary", "arbitrary"), vmem_limit_bytes=VMEM_LIMIT_BYTES),
        name="attn_out",
    )(q, k, k, k, v, v, v, gatt, yc, x2, bias_tab, gn_g, w_out, final_g)


def _dense_bias_table(rpb):
    n_heads, n_row_rel, n_col_rel = rpb.shape
    period = 2 * GRID_W
    fill = jnp.zeros((n_heads, n_row_rel, period - n_col_rel), rpb.dtype)
    g = jnp.concatenate([rpb[..., KW - 1:], fill, rpb[..., :KW - 1]], axis=-1)
    flat = jnp.broadcast_to(g[:, :, None, :], (n_heads, n_row_rel, GRID_W, period))
    flat = flat.reshape(n_heads, n_row_rel, GRID_W * period)[..., :GRID_W * (period - 1)]
    toep = flat.reshape(n_heads, n_row_rel, GRID_W, period - 1)[..., :GRID_W]
    w = jnp.arange(GRID_W)
    c = jnp.arange(GRID_W)
    col_start = jnp.clip(w - KW // 2, 0, GRID_W - KW)
    valid = (c[None, :] >= col_start[:, None]) & (c[None, :] < col_start[:, None] + KW)
    toep = jnp.where(valid[None, None], toep, MASK_VALUE).transpose(0, 2, 1, 3)
    tab = jnp.stack([toep[:, :, MAX_KH - 1 - cls:2 * MAX_KH - 1 - cls, :] for cls in range(MAX_KH)])
    return tab.reshape(MAX_KH, N_HEAD_GROUPS, HEADS_PER_GROUP * GRID_W, WIN_KEYS).astype(jnp.float32)


def kernel(x, ln_g, w_in, b_in, dw_w, dw_b, cln_g, cln_b, pw_w, pw_b, rpb,
           gn_conv_g, gn_att_g, w_out, final_g):
    batch, seq, d_model = x.shape
    depth = w_in.shape[0]
    assert d_model == D_MODEL and seq % TOKEN_TILE == 0 and seq // GRID_W >= MAX_KH
    bf16 = jnp.bfloat16
    h = x.reshape(batch * seq, d_model)
    for l in range(depth):
        last = l == depth - 1
        u, gconv, q, k, v, gatt = _in_proj(h, ln_g[l][None], w_in[l].astype(bf16), b_in[l][None])
        yc = _conv_mix(u, gconv, dw_w[l], dw_b[l][None], cln_g[l][None], cln_b[l][None],
                       pw_w[l].astype(bf16), pw_b[l][None], gn_conv_g[l][None], batch, seq)
        assert last, "multi-layer stacks need an un-normalised attn_out variant"
        h = _attn_out(q, k, v, gatt, yc, h, _dense_bias_table(rpb[l]), gn_att_g[l][None],
                      w_out[l].astype(bf16), final_g[None], batch, seq)
    return h.reshape(batch, seq, d_model)
```

```python
import functools

import jax
import jax.numpy as jnp
from jax import lax
from jax.experimental import pallas as pl
from jax.experimental.pallas import tpu as pltpu

D_MODEL = 1024
CONV_CH = 512
CONV_GROUPS = 8
N_HEADS = 8
HEAD_DIM = 64
ATT_CH = N_HEADS * HEAD_DIM
CONV_WIDTH = 31
CONV_PAD = CONV_WIDTH // 2
GRID_W = 64
MAX_KH = 8
KW = 16
EPS = 1e-6
PROJ_OUT = 3 * CONV_CH + 4 * ATT_CH

SUBLANES = 8
LANES = 128
HALO = 16
TOKEN_TILE = 512
N_LANE_CHUNKS = CONV_CH // LANES
CONV_STRIDE = TOKEN_TILE // SUBLANES + 1
CONV_C_ROWS = SUBLANES * CONV_STRIDE
CONV_U_ROWS = -(-(CONV_C_ROWS + 2 * HALO) // SUBLANES) * SUBLANES
ROWS_PER_TILE = TOKEN_TILE // GRID_W
KV_HALO_ROWS = MAX_KH // 2
KV_HALO = KV_HALO_ROWS * GRID_W
HEADS_PER_GROUP = 4
GROUP_CH = HEADS_PER_GROUP * HEAD_DIM
N_HEAD_GROUPS = N_HEADS // HEADS_PER_GROUP
WIN_KEYS = MAX_KH * GRID_W
MASK_VALUE = -1e30
VMEM_LIMIT_BYTES = 56 * 1024 * 1024


def _silu(z):
    return z * jax.nn.sigmoid(z)


def _group_mean_square(y, n_groups):
    c = y.shape[-1]
    gsz = c // n_groups
    gi = lax.broadcasted_iota(jnp.int32, (c, c), 0) // gsz
    gj = lax.broadcasted_iota(jnp.int32, (c, c), 1) // gsz
    same = (gi == gj).astype(jnp.bfloat16)
    ysq = y * y
    hi = ysq.astype(jnp.bfloat16)
    lo = (ysq - hi.astype(jnp.float32)).astype(jnp.bfloat16)
    ss = (jnp.dot(hi, same, preferred_element_type=jnp.float32)
          + jnp.dot(lo, same, preferred_element_type=jnp.float32))
    return ss * (1.0 / gsz)


def _depthwise_conv(u_ref, c_ref, dww_ref):
    base = HALO - CONV_PAD
    for ch in range(N_LANE_CHUNKS):
        lanes = slice(ch * LANES, (ch + 1) * LANES)
        w = [jnp.broadcast_to(dww_ref[j:j + 1, lanes], (SUBLANES, LANES)) for j in range(CONV_WIDTH)]
        for t0 in range(CONV_STRIDE):
            acc = u_ref[ch, pl.ds(base + t0, SUBLANES, stride=CONV_STRIDE), :] * w[0]
            for j in range(1, CONV_WIDTH):
                acc = acc + u_ref[ch, pl.ds(base + t0 + j, SUBLANES, stride=CONV_STRIDE), :] * w[j]
            c_ref[ch, pl.ds(t0, SUBLANES, stride=CONV_STRIDE), :] = acc


def _proj_conv_kernel(xp_ref, xm_ref, xn_ref, lng_ref, w_ref, b_ref, dww_ref, dwb_ref, clng_ref,
                      clnb_ref, pww_ref, pwb_ref, gng_ref,
                      yc_ref, q_ref, k_ref, v_ref, ga_ref, hn_ref, u_ref, c_ref):
    t = pl.program_id(1)
    n_t = pl.num_programs(1)
    ext = TOKEN_TILE + 2 * HALO

    def rms(x):
        ms = jnp.mean(x * x, axis=-1, keepdims=True)
        return (x * lax.rsqrt(ms + EPS) * lng_ref[...]).astype(hn_ref.dtype)

    hn_ref[0:HALO, :] = rms(xp_ref[...])
    hn_ref[HALO:HALO + TOKEN_TILE, :] = rms(xm_ref[...])
    hn_ref[HALO + TOKEN_TILE:, :] = rms(xn_ref[...])

    def proj(lhs, g):
        cols = slice(g * CONV_CH, (g + 1) * CONV_CH)
        return jnp.dot(lhs, w_ref[:, cols], preferred_element_type=jnp.float32) + b_ref[:, cols]

    hn_ext = hn_ref[...]
    u = proj(hn_ext, 0) * jax.nn.sigmoid(proj(hn_ext, 1))
    row = lax.broadcasted_iota(jnp.int32, (ext, 1), 0)
    in_seq = ((row >= HALO) | (t > 0)) & ((row < HALO + TOKEN_TILE) | (t < n_t - 1))
    u = jnp.where(in_seq, u, 0.0)
    for ch in range(N_LANE_CHUNKS):
        u_ref[ch, 0:ext, :] = u[:, ch * LANES:(ch + 1) * LANES]
        u_ref[ch, ext:, :] = jnp.zeros((CONV_U_ROWS - ext, LANES), jnp.float32)

    hn = hn_ref[HALO:HALO + TOKEN_TILE, :]
    gconv = _silu(proj(hn, 2))
    q_ref[...] = (proj(hn, 3) * (HEAD_DIM ** -0.5)).astype(q_ref.dtype)
    k_ref[...] = proj(hn, 4).astype(k_ref.dtype)
    v_ref[...] = proj(hn, 5).astype(v_ref.dtype)
    ga_ref[...] = _silu(proj(hn, 6)).astype(ga_ref.dtype)

    _depthwise_conv(u_ref, c_ref, dww_ref)
    c = jnp.concatenate([c_ref[ch, 0:TOKEN_TILE, :] for ch in range(N_LANE_CHUNKS)],
                        axis=-1) + dwb_ref[...]
    mu = jnp.mean(c, axis=-1, keepdims=True)
    cc = c - mu
    var = jnp.mean(cc * cc, axis=-1, keepdims=True)
    n = cc * lax.rsqrt(var + EPS) * clng_ref[...] + clnb_ref[...]
    s = _silu(n).astype(jnp.bfloat16)
    y = jnp.dot(s, pww_ref[...], preferred_element_type=jnp.float32) + pwb_ref[...]
    ms = _group_mean_square(y, CONV_GROUPS)
    yc_ref[...] = (y * lax.rsqrt(ms + EPS) * gng_ref[...] * gconv).astype(yc_ref.dtype)


def _proj_conv(x2, ln_g, w_in, b_in, dw_w, dw_b, cln_g, cln_b, pw_w, pw_b, gn_g, batch, seq):
    tiles = seq // TOKEN_TILE
    halo_per_tile = TOKEN_TILE // HALO
    n_halo_blocks = batch * seq // HALO

    def main_map(b, t):
        return (b * tiles + t, 0)

    def prev_map(b, t):
        return (jnp.maximum((b * tiles + t) * halo_per_tile - 1, 0), 0)

    def next_map(b, t):
        return (jnp.minimum((b * tiles + t + 1) * halo_per_tile, n_halo_blocks - 1), 0)

    const = lambda b, t: (0, 0)
    out = jax.ShapeDtypeStruct((batch * seq, CONV_CH), jnp.bfloat16)
    tile = pl.BlockSpec((TOKEN_TILE, CONV_CH), main_map)
    return pl.pallas_call(
        _proj_conv_kernel,
        grid=(batch, tiles),
        in_specs=[
            pl.BlockSpec((HALO, D_MODEL), prev_map),
            pl.BlockSpec((TOKEN_TILE, D_MODEL), main_map),
            pl.BlockSpec((HALO, D_MODEL), next_map),
            pl.BlockSpec((1, D_MODEL), const),
            pl.BlockSpec((D_MODEL, PROJ_OUT), const),
            pl.BlockSpec((1, PROJ_OUT), const),
            pl.BlockSpec((CONV_WIDTH, CONV_CH), const),
            pl.BlockSpec((1, CONV_CH), const),
            pl.BlockSpec((1, CONV_CH), const),
            pl.BlockSpec((1, CONV_CH), const),
            pl.BlockSpec((CONV_CH, CONV_CH), const),
            pl.BlockSpec((1, CONV_CH), const),
            pl.BlockSpec((1, CONV_CH), const),
        ],
        out_specs=[tile] * 5,
        out_shape=[out] * 5,
        scratch_shapes=[
            pltpu.VMEM((TOKEN_TILE + 2 * HALO, D_MODEL), jnp.bfloat16),
            pltpu.VMEM((N_LANE_CHUNKS, CONV_U_ROWS, LANES), jnp.float32),
            pltpu.VMEM((N_LANE_CHUNKS, CONV_C_ROWS, LANES), jnp.float32),
        ],
        compiler_params=pltpu.CompilerParams(
            dimension_semantics=("arbitrary", "arbitrary"), vmem_limit_bytes=VMEM_LIMIT_BYTES),
        name="proj_conv",
    )(x2, x2, x2, ln_g, w_in, b_in, dw_w, dw_b, cln_g, cln_b, pw_w, pw_b, gn_g)


def _attn_out_kernel(q_ref, kp_ref, km_ref, kn_ref, vp_ref, vm_ref, vn_ref, ga_ref, yc_ref,
                     x_ref, bias_ref, gng_ref, wo_ref, fg_ref, o_ref,
                     kbuf_ref, vbuf_ref, ya_ref, *, n_rows):
    t = pl.program_id(1)
    kbuf_ref[0:KV_HALO, :] = kp_ref[...]
    kbuf_ref[KV_HALO:KV_HALO + TOKEN_TILE, :] = km_ref[...]
    kbuf_ref[KV_HALO + TOKEN_TILE:, :] = kn_ref[...]
    vbuf_ref[0:KV_HALO, :] = vp_ref[...]
    vbuf_ref[KV_HALO:KV_HALO + TOKEN_TILE, :] = vm_ref[...]
    vbuf_ref[KV_HALO + TOKEN_TILE:, :] = vn_ref[...]

    lane_head = lax.broadcasted_iota(jnp.int32, (GRID_W, GROUP_CH), 1) // HEAD_DIM
    row0 = t * ROWS_PER_TILE
    for i in range(ROWS_PER_TILE):
        r = row0 + i
        rs = jnp.clip(r - MAX_KH // 2, 0, n_rows - MAX_KH)
        off = pl.multiple_of((rs - row0 + KV_HALO_ROWS) * GRID_W, GRID_W)
        cls = r - rs
        q_row = q_ref[i * GRID_W:(i + 1) * GRID_W, :]
        outs = []
        for gi in range(N_HEAD_GROUPS):
            ch = slice(gi * GROUP_CH, (gi + 1) * GROUP_CH)
            qg = q_row[:, ch]
            qbd = jnp.concatenate(
                [jnp.where(lane_head == h, qg, jnp.zeros_like(qg)) for h in range(HEADS_PER_GROUP)],
                axis=0)
            k_win = kbuf_ref[pl.ds(off, WIN_KEYS), ch]
            v_win = vbuf_ref[pl.ds(off, WIN_KEYS), ch]
            s = lax.dot_general(qbd, k_win, (((1,), (1,)), ((), ())),
                                preferred_element_type=jnp.float32)
            s = s + bias_ref[cls, gi]
            m = jnp.max(s, axis=-1, keepdims=True)
            e = jnp.exp(s - m)
            l = jnp.sum(e, axis=-1, keepdims=True)
            pv = jnp.dot(e.astype(jnp.bfloat16), v_win, preferred_element_type=jnp.float32)
            pv = pv / l
            o = jnp.zeros((GRID_W, GROUP_CH), jnp.float32)
            for h in range(HEADS_PER_GROUP):
                o = o + jnp.where(lane_head == h, pv[h * GRID_W:(h + 1) * GRID_W, :], 0.0)
            outs.append(o)
        ya_ref[i * GRID_W:(i + 1) * GRID_W, :] = jnp.concatenate(outs, axis=-1)

    ya = ya_ref[...]
    ms = _group_mean_square(ya, N_HEADS)
    yan = ya * lax.rsqrt(ms + EPS) * gng_ref[...] * ga_ref[...].astype(jnp.float32)
    h = (x_ref[...]
         + jnp.dot(yc_ref[...], wo_ref[0:CONV_CH, :], preferred_element_type=jnp.float32)
         + jnp.dot(yan.astype(jnp.bfloat16), wo_ref[CONV_CH:, :],
                   preferred_element_type=jnp.float32))
    hms = jnp.mean(h * h, axis=-1, keepdims=True)
    o_ref[...] = (h * lax.rsqrt(hms + EPS) * fg_ref[...]).astype(o_ref.dtype)


def _attn_out(q, k, v, gatt, yc, x2, bias_tab, gn_g, w_out, final_g, batch, seq):
    tiles = seq // TOKEN_TILE
    n_rows = seq // GRID_W
    halo_per_tile = TOKEN_TILE // KV_HALO
    n_halo_blocks = batch * seq // KV_HALO

    def main_map(b, t):
        return (b * tiles + t, 0)

    def prev_map(b, t):
        return (jnp.maximum((b * tiles + t) * halo_per_tile - 1, 0), 0)

    def next_map(b, t):
        return (jnp.minimum((b * tiles + t + 1) * halo_per_tile, n_halo_blocks - 1), 0)

    const2 = lambda b, t: (0, 0)
    tile = pl.BlockSpec((TOKEN_TILE, ATT_CH), main_map)
    halo_p = pl.BlockSpec((KV_HALO, ATT_CH), prev_map)
    halo_n = pl.BlockSpec((KV_HALO, ATT_CH), next_map)
    return pl.pallas_call(
        functools.partial(_attn_out_kernel, n_rows=n_rows),
        grid=(batch, tiles),
        in_specs=[
            tile, halo_p, tile, halo_n, halo_p, tile, halo_n, tile, tile,
            pl.BlockSpec((TOKEN_TILE, D_MODEL), main_map),
            pl.BlockSpec(bias_tab.shape, lambda b, t: (0, 0, 0, 0)),
            pl.BlockSpec((1, ATT_CH), const2),
            pl.BlockSpec((D_MODEL, D_MODEL), const2),
            pl.BlockSpec((1, D_MODEL), const2),
        ],
        out_specs=pl.BlockSpec((TOKEN_TILE, D_MODEL), main_map),
        out_shape=jax.ShapeDtypeStruct((batch * seq, D_MODEL), x2.dtype),
        scratch_shapes=[
            pltpu.VMEM((TOKEN_TILE + 2 * KV_HALO, ATT_CH), jnp.bfloat16),
            pltpu.VMEM((TOKEN_TILE + 2 * KV_HALO, ATT_CH), jnp.bfloat16),
            pltpu.VMEM((TOKEN_TILE, ATT_CH), jnp.float32),
        ],
        compiler_params=pltpu.CompilerParams(
            dimension_semantics=("arbitrary", "arbitrary"), vmem_limit_bytes=VMEM_LIMIT_BYTES),
        name="attn_out",
    )(q, k, k, k, v, v, v, gatt, yc, x2, bias_tab, gn_g, w_out, final_g)


def _dense_bias_table(rpb):
    n_heads, n_row_rel, n_col_rel = rpb.shape
    period = 2 * GRID_W
    fill = jnp.zeros((n_heads, n_row_rel, period - n_col_rel), rpb.dtype)
    g = jnp.concatenate([rpb[..., KW - 1:], fill, rpb[..., :KW - 1]], axis=-1)
    flat = jnp.broadcast_to(g[:, :, None, :], (n_heads, n_row_rel, GRID_W, period))
    flat = flat.reshape(n_heads, n_row_rel, GRID_W * period)[..., :GRID_W * (period - 1)]
    toep = flat.reshape(n_heads, n_row_rel, GRID_W, period - 1)[..., :GRID_W]
    w = jnp.arange(GRID_W)
    c = jnp.arange(GRID_W)
    col_start = jnp.clip(w - KW // 2, 0, GRID_W - KW)
    valid = (c[None, :] >= col_start[:, None]) & (c[None, :] < col_start[:, None] + KW)
    toep = jnp.where(valid[None, None], toep, MASK_VALUE).transpose(0, 2, 1, 3)
    tab = jnp.stack([toep[:, :, MAX_KH - 1 - cls:2 * MAX_KH - 1 - cls, :] for cls in range(MAX_KH)])
    return tab.reshape(MAX_KH, N_HEAD_GROUPS, HEADS_PER_GROUP * GRID_W, WIN_KEYS).astype(jnp.float32)


def kernel(x, ln_g, w_in, b_in, dw_w, dw_b, cln_g, cln_b, pw_w, pw_b, rpb,
           gn_conv_g, gn_att_g, w_out, final_g):
    batch, seq, d_model = x.shape
    depth = w_in.shape[0]
    assert d_model == D_MODEL and seq % TOKEN_TILE == 0 and seq // GRID_W >= MAX_KH
    bf16 = jnp.bfloat16
    h = x.reshape(batch * seq, d_model)
    for l in range(depth):
        last = l == depth - 1
        yc, q, k, v, gatt = _proj_conv(
            h, ln_g[l][None], w_in[l].astype(bf16), b_in[l][None], dw_w[l], dw_b[l][None],
            cln_g[l][None], cln_b[l][None], pw_w[l].astype(bf16), pw_b[l][None],
            gn_conv_g[l][None], batch, seq)
        assert last, "multi-layer stacks need an un-normalised attn_out variant"
        h = _attn_out(q, k, v, gatt, yc, h, _dense_bias_table(rpb[l]), gn_att_g[l][None],
                      w_out[l].astype(bf16), final_g[None], batch, seq)
    return h.reshape(batch, seq, d_model)
```

```python
import functools

import jax
import jax.numpy as jnp
from jax import lax
from jax.experimental import pallas as pl
from jax.experimental.pallas import tpu as pltpu

D_MODEL = 1024
CONV_CH = 512
CONV_GROUPS = 8
N_HEADS = 8
HEAD_DIM = 64
ATT_CH = N_HEADS * HEAD_DIM
CONV_WIDTH = 31
CONV_PAD = CONV_WIDTH // 2
GRID_W = 64
MAX_KH = 8
KW = 16
EPS = 1e-6
PROJ_OUT = 3 * CONV_CH + 4 * ATT_CH

SUBLANES = 8
LANES = 128
HALO = 16
TOKEN_TILE = 512
N_LANE_CHUNKS = CONV_CH // LANES
CONV_STRIDE = TOKEN_TILE // SUBLANES + 1
CONV_C_ROWS = SUBLANES * CONV_STRIDE
CONV_U_ROWS = -(-(CONV_C_ROWS + 2 * HALO) // SUBLANES) * SUBLANES
ROWS_PER_TILE = TOKEN_TILE // GRID_W
KV_HALO_ROWS = MAX_KH // 2
KV_HALO = KV_HALO_ROWS * GRID_W
HEADS_PER_GROUP = 4
GROUP_CH = HEADS_PER_GROUP * HEAD_DIM
N_HEAD_GROUPS = N_HEADS // HEADS_PER_GROUP
WIN_KEYS = MAX_KH * GRID_W
MASK_VALUE = -1e30
LOG2_E = 1.4426950408889634
BIAS_PAIRS = 2 * MAX_KH - 2
VMEM_LIMIT_BYTES = 56 * 1024 * 1024


def _silu(z):
    return z * jax.nn.sigmoid(z)


def _group_mean_square(y, n_groups):
    c = y.shape[-1]
    gsz = c // n_groups
    gi = lax.broadcasted_iota(jnp.int32, (c, c), 0) // gsz
    gj = lax.broadcasted_iota(jnp.int32, (c, c), 1) // gsz
    same = (gi == gj).astype(jnp.bfloat16)
    ss = jnp.dot((y * y).astype(jnp.bfloat16), same, preferred_element_type=jnp.float32)
    return ss * (1.0 / gsz)


def _depthwise_conv(u_ref, c_ref, dww_ref):
    base = HALO - CONV_PAD
    for ch in range(N_LANE_CHUNKS):
        lanes = slice(ch * LANES, (ch + 1) * LANES)
        w = [jnp.broadcast_to(dww_ref[j:j + 1, lanes], (SUBLANES, LANES)) for j in range(CONV_WIDTH)]
        for t0 in range(CONV_STRIDE):
            acc = u_ref[ch, pl.ds(base + t0, SUBLANES, stride=CONV_STRIDE), :] * w[0]
            for j in range(1, CONV_WIDTH):
                acc = acc + u_ref[ch, pl.ds(base + t0 + j, SUBLANES, stride=CONV_STRIDE), :] * w[j]
            c_ref[ch, pl.ds(t0, SUBLANES, stride=CONV_STRIDE), :] = acc


def _proj_conv_kernel(xp_ref, xm_ref, xn_ref, lng_ref, w_ref, b_ref, dww_ref, dwb_ref, clng_ref,
                      clnb_ref, pww_ref, pwb_ref, gng_ref,
                      yc_ref, q_ref, k_ref, v_ref, ga_ref, hn_ref, u_ref, c_ref):
    t = pl.program_id(1)
    n_t = pl.num_programs(1)
    ext = TOKEN_TILE + 2 * HALO

    def rms(x):
        ms = jnp.mean(x * x, axis=-1, keepdims=True)
        return (x * lax.rsqrt(ms + EPS) * lng_ref[...]).astype(hn_ref.dtype)

    hn_ref[0:HALO, :] = rms(xp_ref[...])
    hn_ref[HALO:HALO + TOKEN_TILE, :] = rms(xm_ref[...])
    hn_ref[HALO + TOKEN_TILE:, :] = rms(xn_ref[...])

    def proj(lhs, g):
        cols = slice(g * CONV_CH, (g + 1) * CONV_CH)
        return jnp.dot(lhs, w_ref[:, cols], preferred_element_type=jnp.float32) + b_ref[:, cols]

    hn_ext = hn_ref[...]
    u = proj(hn_ext, 0) * jax.nn.sigmoid(proj(hn_ext, 1))
    row = lax.broadcasted_iota(jnp.int32, (ext, 1), 0)
    in_seq = ((row >= HALO) | (t > 0)) & ((row < HALO + TOKEN_TILE) | (t < n_t - 1))
    u = jnp.where(in_seq, u, 0.0)
    for ch in range(N_LANE_CHUNKS):
        u_ref[ch, 0:ext, :] = u[:, ch * LANES:(ch + 1) * LANES]
        u_ref[ch, ext:, :] = jnp.zeros((CONV_U_ROWS - ext, LANES), jnp.float32)

    hn = hn_ref[HALO:HALO + TOKEN_TILE, :]
    gconv = _silu(proj(hn, 2))
    q_ref[...] = (proj(hn, 3) * (HEAD_DIM ** -0.5 * LOG2_E)).astype(q_ref.dtype)
    k_ref[...] = proj(hn, 4).astype(k_ref.dtype)
    v_ref[...] = proj(hn, 5).astype(v_ref.dtype)
    ga_ref[...] = _silu(proj(hn, 6)).astype(ga_ref.dtype)

    _depthwise_conv(u_ref, c_ref, dww_ref)
    c = jnp.concatenate([c_ref[ch, 0:TOKEN_TILE, :] for ch in range(N_LANE_CHUNKS)],
                        axis=-1) + dwb_ref[...]
    mu = jnp.mean(c, axis=-1, keepdims=True)
    cc = c - mu
    var = jnp.mean(cc * cc, axis=-1, keepdims=True)
    n = cc * lax.rsqrt(var + EPS) * clng_ref[...] + clnb_ref[...]
    s = _silu(n).astype(jnp.bfloat16)
    y = jnp.dot(s, pww_ref[...], preferred_element_type=jnp.float32) + pwb_ref[...]
    ms = _group_mean_square(y, CONV_GROUPS)
    yc_ref[...] = (y * lax.rsqrt(ms + EPS) * gng_ref[...] * gconv).astype(yc_ref.dtype)


def _proj_conv(x2, ln_g, w_in, b_in, dw_w, dw_b, cln_g, cln_b, pw_w, pw_b, gn_g, batch, seq):
    tiles = seq // TOKEN_TILE
    halo_per_tile = TOKEN_TILE // HALO
    n_halo_blocks = batch * seq // HALO

    def main_map(b, t):
        return (b * tiles + t, 0)

    def prev_map(b, t):
        return (jnp.maximum((b * tiles + t) * halo_per_tile - 1, 0), 0)

    def next_map(b, t):
        return (jnp.minimum((b * tiles + t + 1) * halo_per_tile, n_halo_blocks - 1), 0)

    const = lambda b, t: (0, 0)
    out = jax.ShapeDtypeStruct((batch * seq, CONV_CH), jnp.bfloat16)
    tile = pl.BlockSpec((TOKEN_TILE, CONV_CH), main_map)
    return pl.pallas_call(
        _proj_conv_kernel,
        grid=(batch, tiles),
        in_specs=[
            pl.BlockSpec((HALO, D_MODEL), prev_map),
            pl.BlockSpec((TOKEN_TILE, D_MODEL), main_map),
            pl.BlockSpec((HALO, D_MODEL), next_map),
            pl.BlockSpec((1, D_MODEL), const),
            pl.BlockSpec((D_MODEL, PROJ_OUT), const),
            pl.BlockSpec((1, PROJ_OUT), const),
            pl.BlockSpec((CONV_WIDTH, CONV_CH), const),
            pl.BlockSpec((1, CONV_CH), const),
            pl.BlockSpec((1, CONV_CH), const),
            pl.BlockSpec((1, CONV_CH), const),
            pl.BlockSpec((CONV_CH, CONV_CH), const),
            pl.BlockSpec((1, CONV_CH), const),
            pl.BlockSpec((1, CONV_CH), const),
        ],
        out_specs=[tile] * 5,
        out_shape=[out] * 5,
        scratch_shapes=[
            pltpu.VMEM((TOKEN_TILE + 2 * HALO, D_MODEL), jnp.bfloat16),
            pltpu.VMEM((N_LANE_CHUNKS, CONV_U_ROWS, LANES), jnp.float32),
            pltpu.VMEM((N_LANE_CHUNKS, CONV_C_ROWS, LANES), jnp.float32),
        ],
        compiler_params=pltpu.CompilerParams(
            dimension_semantics=("arbitrary", "arbitrary"), vmem_limit_bytes=VMEM_LIMIT_BYTES),
        name="proj_conv",
    )(x2, x2, x2, ln_g, w_in, b_in, dw_w, dw_b, cln_g, cln_b, pw_w, pw_b, gn_g)


def _attn_out_kernel(q_ref, kp_ref, km_ref, kn_ref, vp_ref, vm_ref, vn_ref, ga_ref, yc_ref,
                     x_ref, rel_ref, gng_ref, wo_ref, fg_ref, o_ref,
                     kbuf_ref, vbuf_ref, ya_ref, bias_ref, *, n_rows):
    t = pl.program_id(1)

    @pl.when((pl.program_id(0) == 0) & (t == 0))
    def _build_bias():
        w = lax.broadcasted_iota(jnp.int32, (GRID_W, 2 * GRID_W), 0)
        c = lax.broadcasted_iota(jnp.int32, (GRID_W, 2 * GRID_W), 1) % GRID_W
        start = jnp.clip(w - KW // 2, 0, GRID_W - KW)
        valid = (c >= start) & (c < start + KW)
        for h in range(N_HEADS):
            for a in range(BIAS_PAIRS):
                rows = jnp.broadcast_to(rel_ref[h, a:a + 1, :], (GRID_W, 2 * GRID_W))
                toep = pltpu.roll(rows, 0, 1, stride=1, stride_axis=0)
                bias_ref[h, a] = jnp.where(valid, toep * LOG2_E, MASK_VALUE)

    kbuf_ref[0:KV_HALO, :] = kp_ref[...]
    kbuf_ref[KV_HALO:KV_HALO + TOKEN_TILE, :] = km_ref[...]
    kbuf_ref[KV_HALO + TOKEN_TILE:, :] = kn_ref[...]
    vbuf_ref[0:KV_HALO, :] = vp_ref[...]
    vbuf_ref[KV_HALO:KV_HALO + TOKEN_TILE, :] = vm_ref[...]
    vbuf_ref[KV_HALO + TOKEN_TILE:, :] = vn_ref[...]

    lane_head = lax.broadcasted_iota(jnp.int32, (GRID_W, GROUP_CH), 1) // HEAD_DIM
    row0 = t * ROWS_PER_TILE
    for i in range(ROWS_PER_TILE):
        r = row0 + i
        rs = jnp.clip(r - MAX_KH // 2, 0, n_rows - MAX_KH)
        off = pl.multiple_of((rs - row0 + KV_HALO_ROWS) * GRID_W, GRID_W)
        cls = r - rs
        q_row = q_ref[i * GRID_W:(i + 1) * GRID_W, :]
        outs = []
        for gi in range(N_HEAD_GROUPS):
            ch = slice(gi * GROUP_CH, (gi + 1) * GROUP_CH)
            qg = q_row[:, ch]
            qbd = jnp.concatenate(
                [jnp.where(lane_head == h, qg, jnp.zeros_like(qg)) for h in range(HEADS_PER_GROUP)],
                axis=0)
            k_win = kbuf_ref[pl.ds(off, WIN_KEYS), ch]
            v_win = vbuf_ref[pl.ds(off, WIN_KEYS), ch]
            s = lax.dot_general(qbd, k_win, (((1,), (1,)), ((), ())),
                                preferred_element_type=jnp.float32)
            s = s + jnp.concatenate(
                [jnp.concatenate([bias_ref[gi * HEADS_PER_GROUP + h, 2 * p - cls + MAX_KH - 1]
                                  for p in range(MAX_KH // 2)], axis=1)
                 for h in range(HEADS_PER_GROUP)], axis=0)
            m = jnp.max(s, axis=-1, keepdims=True)
            e = jnp.exp2(s - m)
            l = jnp.sum(e, axis=-1, keepdims=True)
            pv = jnp.dot(e.astype(jnp.bfloat16), v_win, preferred_element_type=jnp.float32)
            pv = pv / l
            o = jnp.zeros((GRID_W, GROUP_CH), jnp.float32)
            for h in range(HEADS_PER_GROUP):
                o = o + jnp.where(lane_head == h, pv[h * GRID_W:(h + 1) * GRID_W, :], 0.0)
            outs.append(o)
        ya_ref[i * GRID_W:(i + 1) * GRID_W, :] = jnp.concatenate(outs, axis=-1)

    ya = ya_ref[...]
    ms = _group_mean_square(ya, N_HEADS)
    yan = ya * lax.rsqrt(ms + EPS) * gng_ref[...] * ga_ref[...].astype(jnp.float32)
    h = (x_ref[...]
         + jnp.dot(yc_ref[...], wo_ref[0:CONV_CH, :], preferred_element_type=jnp.float32)
         + jnp.dot(yan.astype(jnp.bfloat16), wo_ref[CONV_CH:, :],
                   preferred_element_type=jnp.float32))
    hms = jnp.mean(h * h, axis=-1, keepdims=True)
    o_ref[...] = (h * lax.rsqrt(hms + EPS) * fg_ref[...]).astype(o_ref.dtype)


def _attn_out(q, k, v, gatt, yc, x2, rel_rows, gn_g, w_out, final_g, batch, seq):
    tiles = seq // TOKEN_TILE
    n_rows = seq // GRID_W
    halo_per_tile = TOKEN_TILE // KV_HALO
    n_halo_blocks = batch * seq // KV_HALO

    def main_map(b, t):
        return (b * tiles + t, 0)

    def prev_map(b, t):
        return (jnp.maximum((b * tiles + t) * halo_per_tile - 1, 0), 0)

    def next_map(b, t):
        return (jnp.minimum((b * tiles + t + 1) * halo_per_tile, n_halo_blocks - 1), 0)

    const2 = lambda b, t: (0, 0)
    tile = pl.BlockSpec((TOKEN_TILE, ATT_CH), main_map)
    halo_p = pl.BlockSpec((KV_HALO, ATT_CH), prev_map)
    halo_n = pl.BlockSpec((KV_HALO, ATT_CH), next_map)
    return pl.pallas_call(
        functools.partial(_attn_out_kernel, n_rows=n_rows),
        grid=(batch, tiles),
        in_specs=[
            tile, halo_p, tile, halo_n, halo_p, tile, halo_n, tile, tile,
            pl.BlockSpec((TOKEN_TILE, D_MODEL), main_map),
            pl.BlockSpec(rel_rows.shape, lambda b, t: (0, 0, 0)),
            pl.BlockSpec((1, ATT_CH), const2),
            pl.BlockSpec((D_MODEL, D_MODEL), const2),
            pl.BlockSpec((1, D_MODEL), const2),
        ],
        out_specs=pl.BlockSpec((TOKEN_TILE, D_MODEL), main_map),
        out_shape=jax.ShapeDtypeStruct((batch * seq, D_MODEL), x2.dtype),
        scratch_shapes=[
            pltpu.VMEM((TOKEN_TILE + 2 * KV_HALO, ATT_CH), jnp.bfloat16),
            pltpu.VMEM((TOKEN_TILE + 2 * KV_HALO, ATT_CH), jnp.bfloat16),
            pltpu.VMEM((TOKEN_TILE, ATT_CH), jnp.float32),
            pltpu.VMEM((N_HEADS, BIAS_PAIRS, GRID_W, 2 * GRID_W), jnp.float32),
        ],
        compiler_params=pltpu.CompilerParams(
            dimension_semantics=("arbitrary", "arbitrary"), vmem_limit_bytes=VMEM_LIMIT_BYTES),
        name="attn_out",
    )(q, k, k, k, v, v, v, gatt, yc, x2, rel_rows, gn_g, w_out, final_g)


def _paired_rel_rows(rpb):
    n_heads, n_row_rel, n_col_rel = rpb.shape
    gap = jnp.zeros((n_heads, BIAS_PAIRS, GRID_W - n_col_rel), rpb.dtype)
    first, second = rpb[:, :BIAS_PAIRS], rpb[:, 1:]
    return jnp.concatenate(
        [first[..., KW - 1:], gap, second, gap, first[..., :KW - 1]], axis=-1).astype(jnp.float32)


def kernel(x, ln_g, w_in, b_in, dw_w, dw_b, cln_g, cln_b, pw_w, pw_b, rpb,
           gn_conv_g, gn_att_g, w_out, final_g):
    batch, seq, d_model = x.shape
    depth = w_in.shape[0]
    assert d_model == D_MODEL and seq % TOKEN_TILE == 0 and seq // GRID_W >= MAX_KH
    bf16 = jnp.bfloat16
    h = x.reshape(batch * seq, d_model)
    for l in range(depth):
        last = l == depth - 1
        yc, q, k, v, gatt = _proj_conv(
            h, ln_g[l][None], w_in[l].astype(bf16), b_in[l][None], dw_w[l], dw_b[l][None],
            cln_g[l][None], cln_b[l][None], pw_w[l].astype(bf16), pw_b[l][None],
            gn_conv_g[l][None], batch, seq)
        assert last, "multi-layer stacks need an un-normalised attn_out variant"
        h = _attn_out(q, k, v, gatt, yc, h, _paired_rel_rows(rpb[l]), gn_att_g[l][None],
                      w_out[l].astype(bf16), final_g[None], batch, seq)
    return h.reshape(batch, seq, d_model)
```

```python
import functools

import jax
import jax.numpy as jnp
from jax import lax
from jax.experimental import pallas as pl
from jax.experimental.pallas import tpu as pltpu

D_MODEL = 1024
CONV_CH = 512
CONV_GROUPS = 8
N_HEADS = 8
HEAD_DIM = 64
ATT_CH = N_HEADS * HEAD_DIM
CONV_WIDTH = 31
CONV_PAD = CONV_WIDTH // 2
GRID_W = 64
MAX_KH = 8
KW = 16
EPS = 1e-6
PROJ_OUT = 3 * CONV_CH + 4 * ATT_CH

SUBLANES = 8
LANES = 128
HALO = 16
TOKEN_TILE = 512
N_LANE_CHUNKS = CONV_CH // LANES
CONV_STRIDE = TOKEN_TILE // SUBLANES + 1
CONV_C_ROWS = SUBLANES * CONV_STRIDE
CONV_TAP_GROUP = 8
CONV_U_ROWS = -(-(CONV_C_ROWS + 2 * HALO) // SUBLANES) * SUBLANES
ROWS_PER_TILE = TOKEN_TILE // GRID_W
KV_HALO_ROWS = MAX_KH // 2
KV_HALO = KV_HALO_ROWS * GRID_W
HEADS_PER_GROUP = 4
GROUP_CH = HEADS_PER_GROUP * HEAD_DIM
N_HEAD_GROUPS = N_HEADS // HEADS_PER_GROUP
WIN_KEYS = MAX_KH * GRID_W
MASK_VALUE = -1e30
LOG2_E = 1.4426950408889634
BIAS_PAIRS = 2 * MAX_KH - 2
VMEM_LIMIT_BYTES = 56 * 1024 * 1024


def _silu(z):
    return z * jax.nn.sigmoid(z)


def _group_mean_square(y, n_groups):
    c = y.shape[-1]
    gsz = c // n_groups
    gi = lax.broadcasted_iota(jnp.int32, (c, c), 0) // gsz
    gj = lax.broadcasted_iota(jnp.int32, (c, c), 1) // gsz
    same = (gi == gj).astype(jnp.bfloat16)
    ss = jnp.dot((y * y).astype(jnp.bfloat16), same, preferred_element_type=jnp.float32)
    return ss * (1.0 / gsz)


def _depthwise_conv(u_ref, c_ref, dww_ref, t0_lo, t0_hi):
    base = HALO - CONV_PAD
    for ch in range(N_LANE_CHUNKS):
        lanes = slice(ch * LANES, (ch + 1) * LANES)
        acc = [None] * (t0_hi - t0_lo)
        for j_lo in range(0, CONV_WIDTH, CONV_TAP_GROUP):
            j_hi = min(j_lo + CONV_TAP_GROUP, CONV_WIDTH)
            w = {j: jnp.broadcast_to(dww_ref[j:j + 1, lanes], (SUBLANES, LANES))
                 for j in range(j_lo, j_hi)}
            for d in range(t0_lo + j_lo, t0_hi + j_hi - 1):
                data = u_ref[ch, pl.ds(base + d, SUBLANES, stride=CONV_STRIDE), :]
                for t0 in range(max(t0_lo, d - j_hi + 1), min(t0_hi, d - j_lo + 1)):
                    term = data * w[d - t0]
                    acc[t0 - t0_lo] = term if acc[t0 - t0_lo] is None else acc[t0 - t0_lo] + term
        for t0 in range(t0_lo, t0_hi):
            c_ref[ch, pl.ds(t0, SUBLANES, stride=CONV_STRIDE), :] = acc[t0 - t0_lo]


def _proj_conv_kernel(xp_ref, xm_ref, xn_ref, lng_ref, w_ref, b_ref, dww_ref, dwb_ref, clng_ref,
                      clnb_ref, pww_ref, pwb_ref, gng_ref,
                      yc_ref, q_ref, k_ref, v_ref, ga_ref, hn_ref, u_ref, c_ref, gc_ref,
                      *, n_tiles, tiles_per_seq):
    s = pl.program_id(0)
    t = lax.rem(jnp.minimum(s, n_tiles - 1), tiles_per_seq)
    ext = TOKEN_TILE + 2 * HALO

    @pl.when(s == 0)
    def _init():
        c_ref[...] = jnp.zeros(c_ref.shape, c_ref.dtype)
        gc_ref[...] = jnp.zeros(gc_ref.shape, gc_ref.dtype)

    def conv_tail():
        c = jnp.concatenate([c_ref[ch, 0:TOKEN_TILE, :] for ch in range(N_LANE_CHUNKS)],
                            axis=-1) + dwb_ref[...]
        mu = jnp.mean(c, axis=-1, keepdims=True)
        cc = c - mu
        var = jnp.mean(cc * cc, axis=-1, keepdims=True)
        n = cc * lax.rsqrt(var + EPS) * clng_ref[...] + clnb_ref[...]
        sw = _silu(n).astype(jnp.bfloat16)
        y = jnp.dot(sw, pww_ref[...], preferred_element_type=jnp.float32) + pwb_ref[...]
        ms = _group_mean_square(y, CONV_GROUPS)
        yc_ref[...] = (y * lax.rsqrt(ms + EPS) * gng_ref[...] * gc_ref[...]).astype(yc_ref.dtype)

    def rms(x):
        ms = jnp.mean(x * x, axis=-1, keepdims=True)
        return (x * lax.rsqrt(ms + EPS) * lng_ref[...]).astype(hn_ref.dtype)

    def proj(lhs, g):
        cols = slice(g * CONV_CH, (g + 1) * CONV_CH)
        return jnp.dot(lhs, w_ref[:, cols], preferred_element_type=jnp.float32) + b_ref[:, cols]

    hn_ref[0:HALO, :] = rms(xp_ref[...])
    hn_ref[HALO:HALO + TOKEN_TILE, :] = rms(xm_ref[...])
    hn_ref[HALO + TOKEN_TILE:, :] = rms(xn_ref[...])
    hn_ext = hn_ref[...]
    u = proj(hn_ext, 0) * jax.nn.sigmoid(proj(hn_ext, 1))
    row = lax.broadcasted_iota(jnp.int32, (ext, 1), 0)
    in_seq = ((row >= HALO) | (t > 0)) & ((row < HALO + TOKEN_TILE) | (t < tiles_per_seq - 1))
    u = jnp.where(in_seq, u, 0.0)
    conv_tail()
    for ch in range(N_LANE_CHUNKS):
        u_ref[ch, 0:ext, :] = u[:, ch * LANES:(ch + 1) * LANES]
        u_ref[ch, ext:, :] = jnp.zeros((CONV_U_ROWS - ext, LANES), jnp.float32)

    def store_gconv(p):
        gc_ref[...] = _silu(p)

    def store_q(p):
        q_ref[...] = (p * (HEAD_DIM ** -0.5 * LOG2_E)).astype(q_ref.dtype)

    def store_k(p):
        k_ref[...] = p.astype(k_ref.dtype)

    def store_v(p):
        v_ref[...] = p.astype(v_ref.dtype)

    def store_gatt(p):
        ga_ref[...] = _silu(p).astype(ga_ref.dtype)

    stages = [store_gconv, store_q, store_k, store_v, store_gatt]
    always = s >= 0
    per_stage = -(-CONV_STRIDE // len(stages))
    for i, store in enumerate(stages):
        @pl.when(always)
        def _stage(i=i, store=store):
            _depthwise_conv(u_ref, c_ref, dww_ref, i * per_stage, min((i + 1) * per_stage, CONV_STRIDE))
            store(proj(hn_ref[HALO:HALO + TOKEN_TILE, :], 2 + i))


def _proj_conv(x2, ln_g, w_in, b_in, dw_w, dw_b, cln_g, cln_b, pw_w, pw_b, gn_g, batch, seq):
    tiles_per_seq = seq // TOKEN_TILE
    n_tiles = batch * tiles_per_seq
    halo_per_tile = TOKEN_TILE // HALO
    n_halo_blocks = batch * seq // HALO

    def tile_of(s):
        return jnp.minimum(s, n_tiles - 1)

    def main_map(s):
        return (tile_of(s), 0)

    def prev_map(s):
        return (jnp.maximum(tile_of(s) * halo_per_tile - 1, 0), 0)

    def next_map(s):
        return (jnp.minimum((tile_of(s) + 1) * halo_per_tile, n_halo_blocks - 1), 0)

    def tail_map(s):
        return (jnp.maximum(s - 1, 0), 0)

    const = lambda s: (0, 0)
    out = jax.ShapeDtypeStruct((batch * seq, CONV_CH), jnp.bfloat16)
    tile = pl.BlockSpec((TOKEN_TILE, CONV_CH), main_map)
    return pl.pallas_call(
        functools.partial(_proj_conv_kernel, n_tiles=n_tiles, tiles_per_seq=tiles_per_seq),
        grid=(n_tiles + 1,),
        in_specs=[
            pl.BlockSpec((HALO, D_MODEL), prev_map),
            pl.BlockSpec((TOKEN_TILE, D_MODEL), main_map),
            pl.BlockSpec((HALO, D_MODEL), next_map),
            pl.BlockSpec((1, D_MODEL), const),
            pl.BlockSpec((D_MODEL, PROJ_OUT), const),
            pl.BlockSpec((1, PROJ_OUT), const),
            pl.BlockSpec((CONV_WIDTH, CONV_CH), const),
            pl.BlockSpec((1, CONV_CH), const),
            pl.BlockSpec((1, CONV_CH), const),
            pl.BlockSpec((1, CONV_CH), const),
            pl.BlockSpec((CONV_CH, CONV_CH), const),
            pl.BlockSpec((1, CONV_CH), const),
            pl.BlockSpec((1, CONV_CH), const),
        ],
        out_specs=[pl.BlockSpec((TOKEN_TILE, CONV_CH), tail_map)] + [tile] * 4,
        out_shape=[out] * 5,
        scratch_shapes=[
            pltpu.VMEM((TOKEN_TILE + 2 * HALO, D_MODEL), jnp.bfloat16),
            pltpu.VMEM((N_LANE_CHUNKS, CONV_U_ROWS, LANES), jnp.float32),
            pltpu.VMEM((N_LANE_CHUNKS, CONV_C_ROWS, LANES), jnp.float32),
            pltpu.VMEM((TOKEN_TILE, CONV_CH), jnp.float32),
        ],
        compiler_params=pltpu.CompilerParams(
            dimension_semantics=("arbitrary",), vmem_limit_bytes=VMEM_LIMIT_BYTES),
        name="proj_conv",
    )(x2, x2, x2, ln_g, w_in, b_in, dw_w, dw_b, cln_g, cln_b, pw_w, pw_b, gn_g)


def _attn_out_kernel(q_ref, kp_ref, km_ref, kn_ref, vp_ref, vm_ref, vn_ref, ga_ref, yc_ref,
                     x_ref, rel_ref, gng_ref, wo_ref, fg_ref, o_ref,
                     kbuf_ref, vbuf_ref, ya_ref, bias_ref, *, n_rows):
    t = pl.program_id(1)

    @pl.when((pl.program_id(0) == 0) & (t == 0))
    def _build_bias():
        w = lax.broadcasted_iota(jnp.int32, (GRID_W, 2 * GRID_W), 0)
        c = lax.broadcasted_iota(jnp.int32, (GRID_W, 2 * GRID_W), 1) % GRID_W
        start = jnp.clip(w - KW // 2, 0, GRID_W - KW)
        valid = (c >= start) & (c < start + KW)
        for h in range(N_HEADS):
            for a in range(BIAS_PAIRS):
                rows = jnp.broadcast_to(rel_ref[h, a:a + 1, :], (GRID_W, 2 * GRID_W))
                toep = pltpu.roll(rows, 0, 1, stride=1, stride_axis=0)
                bias_ref[h, a] = jnp.where(valid, toep * LOG2_E, MASK_VALUE)

    kbuf_ref[0:KV_HALO, :] = kp_ref[...]
    kbuf_ref[KV_HALO:KV_HALO + TOKEN_TILE, :] = km_ref[...]
    kbuf_ref[KV_HALO + TOKEN_TILE:, :] = kn_ref[...]
    vbuf_ref[0:KV_HALO, :] = vp_ref[...]
    vbuf_ref[KV_HALO:KV_HALO + TOKEN_TILE, :] = vm_ref[...]
    vbuf_ref[KV_HALO + TOKEN_TILE:, :] = vn_ref[...]

    lane_head = lax.broadcasted_iota(jnp.int32, (GRID_W, GROUP_CH), 1) // HEAD_DIM
    row0 = t * ROWS_PER_TILE
    for i in range(ROWS_PER_TILE):
        r = row0 + i
        rs = jnp.clip(r - MAX_KH // 2, 0, n_rows - MAX_KH)
        off = pl.multiple_of((rs - row0 + KV_HALO_ROWS) * GRID_W, GRID_W)
        cls = r - rs
        q_row = q_ref[i * GRID_W:(i + 1) * GRID_W, :]
        outs = []
        for gi in range(N_HEAD_GROUPS):
            ch = slice(gi * GROUP_CH, (gi + 1) * GROUP_CH)
            qg = q_row[:, ch]
            qbd = jnp.concatenate(
                [jnp.where(lane_head == h, qg, jnp.zeros_like(qg)) for h in range(HEADS_PER_GROUP)],
                axis=0)
            k_win = kbuf_ref[pl.ds(off, WIN_KEYS), ch]
            v_win = vbuf_ref[pl.ds(off, WIN_KEYS), ch]
            s = lax.dot_general(qbd, k_win, (((1,), (1,)), ((), ())),
                                preferred_element_type=jnp.float32)
            s = s + jnp.concatenate(
                [jnp.concatenate([bias_ref[gi * HEADS_PER_GROUP + h, 2 * p - cls + MAX_KH - 1]
                                  for p in range(MAX_KH // 2)], axis=1)
                 for h in range(HEADS_PER_GROUP)], axis=0)
            m = jnp.max(s, axis=-1, keepdims=True)
            e = jnp.exp2(s - m)
            l = jnp.sum(e, axis=-1, keepdims=True)
            pv = jnp.dot(e.astype(jnp.bfloat16), v_win, preferred_element_type=jnp.float32)
            pv = pv / l
            o = jnp.zeros((GRID_W, GROUP_CH), jnp.float32)
            for h in range(HEADS_PER_GROUP):
                o = o + jnp.where(lane_head == h, pv[h * GRID_W:(h + 1) * GRID_W, :], 0.0)
            outs.append(o)
        ya_ref[i * GRID_W:(i + 1) * GRID_W, :] = jnp.concatenate(outs, axis=-1)

    ya = ya_ref[...]
    ms = _group_mean_square(ya, N_HEADS)
    yan = ya * lax.rsqrt(ms + EPS) * gng_ref[...] * ga_ref[...].astype(jnp.float32)
    h = (x_ref[...]
         + jnp.dot(yc_ref[...], wo_ref[0:CONV_CH, :], preferred_element_type=jnp.float32)
         + jnp.dot(yan.astype(jnp.bfloat16), wo_ref[CONV_CH:, :],
                   preferred_element_type=jnp.float32))
    hms = jnp.mean(h * h, axis=-1, keepdims=True)
    o_ref[...] = (h * lax.rsqrt(hms + EPS) * fg_ref[...]).astype(o_ref.dtype)


def _attn_out(q, k, v, gatt, yc, x2, rel_rows, gn_g, w_out, final_g, batch, seq):
    tiles = seq // TOKEN_TILE
    n_rows = seq // GRID_W
    halo_per_tile = TOKEN_TILE // KV_HALO
    n_halo_blocks = batch * seq // KV_HALO

    def main_map(b, t):
        return (b * tiles + t, 0)

    def prev_map(b, t):
        return (jnp.maximum((b * tiles + t) * halo_per_tile - 1, 0), 0)

    def next_map(b, t):
        return (jnp.minimum((b * tiles + t + 1) * halo_per_tile, n_halo_blocks - 1), 0)

    const2 = lambda b, t: (0, 0)
    tile = pl.BlockSpec((TOKEN_TILE, ATT_CH), main_map)
    halo_p = pl.BlockSpec((KV_HALO, ATT_CH), prev_map)
    halo_n = pl.BlockSpec((KV_HALO, ATT_CH), next_map)
    return pl.pallas_call(
        functools.partial(_attn_out_kernel, n_rows=n_rows),
        grid=(batch, tiles),
        in_specs=[
            tile, halo_p, tile, halo_n, halo_p, tile, halo_n, tile, tile,
            pl.BlockSpec((TOKEN_TILE, D_MODEL), main_map),
            pl.BlockSpec(rel_rows.shape, lambda b, t: (0, 0, 0)),
            pl.BlockSpec((1, ATT_CH), const2),
            pl.BlockSpec((D_MODEL, D_MODEL), const2),
            pl.BlockSpec((1, D_MODEL), const2),
        ],
        out_specs=pl.BlockSpec((TOKEN_TILE, D_MODEL), main_map),
        out_shape=jax.ShapeDtypeStruct((batch * seq, D_MODEL), x2.dtype),
        scratch_shapes=[
            pltpu.VMEM((TOKEN_TILE + 2 * KV_HALO, ATT_CH), jnp.bfloat16),
            pltpu.VMEM((TOKEN_TILE + 2 * KV_HALO, ATT_CH), jnp.bfloat16),
            pltpu.VMEM((TOKEN_TILE, ATT_CH), jnp.float32),
            pltpu.VMEM((N_HEADS, BIAS_PAIRS, GRID_W, 2 * GRID_W), jnp.float32),
        ],
        compiler_params=pltpu.CompilerParams(
            dimension_semantics=("arbitrary", "arbitrary"), vmem_limit_bytes=VMEM_LIMIT_BYTES),
        name="attn_out",
    )(q, k, k, k, v, v, v, gatt, yc, x2, rel_rows, gn_g, w_out, final_g)


def _paired_rel_rows(rpb):
    n_heads, n_row_rel, n_col_rel = rpb.shape
    gap = jnp.zeros((n_heads, BIAS_PAIRS, GRID_W - n_col_rel), rpb.dtype)
    first, second = rpb[:, :BIAS_PAIRS], rpb[:, 1:]
    return jnp.concatenate(
        [first[..., KW - 1:], gap, second, gap, first[..., :KW - 1]], axis=-1).astype(jnp.float32)


def kernel(x, ln_g, w_in, b_in, dw_w, dw_b, cln_g, cln_b, pw_w, pw_b, rpb,
           gn_conv_g, gn_att_g, w_out, final_g):
    batch, seq, d_model = x.shape
    depth = w_in.shape[0]
    assert d_model == D_MODEL and seq % TOKEN_TILE == 0 and seq // GRID_W >= MAX_KH
    bf16 = jnp.bfloat16
    h = x.reshape(batch * seq, d_model)
    for l in range(depth):
        last = l == depth - 1
        yc, q, k, v, gatt = _proj_conv(
            h, ln_g[l][None], w_in[l].astype(bf16), b_in[l][None], dw_w[l], dw_b[l][None],
            cln_g[l][None], cln_b[l][None], pw_w[l].astype(bf16), pw_b[l][None],
            gn_conv_g[l][None], batch, seq)
        assert last, "multi-layer stacks need an un-normalised attn_out variant"
        h = _attn_out(q, k, v, gatt, yc, h, _paired_rel_rows(rpb[l]), gn_att_g[l][None],
                      w_out[l].astype(bf16), final_g[None], batch, seq)
    return h.reshape(batch, seq, d_model)
```

```python
import functools

import jax
import jax.numpy as jnp
from jax import lax
from jax.experimental import pallas as pl
from jax.experimental.pallas import tpu as pltpu

D_MODEL = 1024
CONV_CH = 512
CONV_GROUPS = 8
N_HEADS = 8
HEAD_DIM = 64
ATT_CH = N_HEADS * HEAD_DIM
CONV_WIDTH = 31
CONV_PAD = CONV_WIDTH // 2
GRID_W = 64
MAX_KH = 8
KW = 16
EPS = 1e-6
PROJ_OUT = 3 * CONV_CH + 4 * ATT_CH

SUBLANES = 8
LANES = 128
HALO = 16
PROJ_TILE = 1024
TOKEN_TILE = 512
N_LANE_CHUNKS = CONV_CH // LANES
CONV_STRIDE = PROJ_TILE // SUBLANES + 1
CONV_C_ROWS = SUBLANES * CONV_STRIDE
CONV_U_ROWS = -(-(CONV_C_ROWS + 2 * HALO) // SUBLANES) * SUBLANES
ROWS_PER_TILE = TOKEN_TILE // GRID_W
KV_HALO_ROWS = MAX_KH // 2
KV_HALO = KV_HALO_ROWS * GRID_W
HEADS_PER_GROUP = 4
GROUP_CH = HEADS_PER_GROUP * HEAD_DIM
N_HEAD_GROUPS = N_HEADS // HEADS_PER_GROUP
WIN_KEYS = MAX_KH * GRID_W
MASK_VALUE = -1e30
LOG2_E = 1.4426950408889634
BIAS_PAIRS = 2 * MAX_KH - 2
VMEM_LIMIT_BYTES = 56 * 1024 * 1024


def _silu(z):
    return z * jax.nn.sigmoid(z)


def _group_mean_square(y, n_groups):
    c = y.shape[-1]
    gsz = c // n_groups
    gi = lax.broadcasted_iota(jnp.int32, (c, c), 0) // gsz
    gj = lax.broadcasted_iota(jnp.int32, (c, c), 1) // gsz
    same = (gi == gj).astype(jnp.bfloat16)
    ss = jnp.dot((y * y).astype(jnp.bfloat16), same, preferred_element_type=jnp.float32)
    return ss * (1.0 / gsz)


def _depthwise_conv(u_ref, c_ref, dww_ref):
    base = HALO - CONV_PAD
    for ch in range(N_LANE_CHUNKS):
        lanes = slice(ch * LANES, (ch + 1) * LANES)
        w = [jnp.broadcast_to(dww_ref[j:j + 1, lanes], (SUBLANES, LANES)) for j in range(CONV_WIDTH)]
        for t0 in range(CONV_STRIDE):
            acc = u_ref[ch, pl.ds(base + t0, SUBLANES, stride=CONV_STRIDE), :] * w[0]
            for j in range(1, CONV_WIDTH):
                acc = acc + u_ref[ch, pl.ds(base + t0 + j, SUBLANES, stride=CONV_STRIDE), :] * w[j]
            c_ref[ch, pl.ds(t0, SUBLANES, stride=CONV_STRIDE), :] = acc


def _proj_conv_kernel(xp_ref, xm_ref, xn_ref, lng_ref, w_ref, b_ref, dww_ref, dwb_ref, clng_ref,
                      clnb_ref, pww_ref, pwb_ref, gng_ref,
                      yc_ref, q_ref, k_ref, v_ref, ga_ref, hn_ref, u_ref, c_ref):
    t = pl.program_id(1)
    n_t = pl.num_programs(1)
    ext = PROJ_TILE + 2 * HALO

    def rms(x):
        ms = jnp.mean(x * x, axis=-1, keepdims=True)
        return (x * lax.rsqrt(ms + EPS) * lng_ref[...]).astype(hn_ref.dtype)

    hn_ref[0:HALO, :] = rms(xp_ref[...])
    hn_ref[HALO:HALO + PROJ_TILE, :] = rms(xm_ref[...])
    hn_ref[HALO + PROJ_TILE:, :] = rms(xn_ref[...])

    def proj(lhs, g):
        cols = slice(g * CONV_CH, (g + 1) * CONV_CH)
        return jnp.dot(lhs, w_ref[:, cols], preferred_element_type=jnp.float32) + b_ref[:, cols]

    hn_ext = hn_ref[...]
    u = proj(hn_ext, 0) * jax.nn.sigmoid(proj(hn_ext, 1))
    row = lax.broadcasted_iota(jnp.int32, (ext, 1), 0)
    in_seq = ((row >= HALO) | (t > 0)) & ((row < HALO + PROJ_TILE) | (t < n_t - 1))
    u = jnp.where(in_seq, u, 0.0)
    for ch in range(N_LANE_CHUNKS):
        u_ref[ch, 0:ext, :] = u[:, ch * LANES:(ch + 1) * LANES]
        u_ref[ch, ext:, :] = jnp.zeros((CONV_U_ROWS - ext, LANES), jnp.float32)

    hn = hn_ref[HALO:HALO + PROJ_TILE, :]
    gconv = _silu(proj(hn, 2))
    q_ref[...] = (proj(hn, 3) * (HEAD_DIM ** -0.5 * LOG2_E)).astype(q_ref.dtype)
    k_ref[...] = proj(hn, 4).astype(k_ref.dtype)
    v_ref[...] = proj(hn, 5).astype(v_ref.dtype)
    ga_ref[...] = _silu(proj(hn, 6)).astype(ga_ref.dtype)

    _depthwise_conv(u_ref, c_ref, dww_ref)
    c = jnp.concatenate([c_ref[ch, 0:PROJ_TILE, :] for ch in range(N_LANE_CHUNKS)],
                        axis=-1) + dwb_ref[...]
    mu = jnp.mean(c, axis=-1, keepdims=True)
    cc = c - mu
    var = jnp.mean(cc * cc, axis=-1, keepdims=True)
    n = cc * lax.rsqrt(var + EPS) * clng_ref[...] + clnb_ref[...]
    s = _silu(n).astype(jnp.bfloat16)
    y = jnp.dot(s, pww_ref[...], preferred_element_type=jnp.float32) + pwb_ref[...]
    ms = _group_mean_square(y, CONV_GROUPS)
    yc_ref[...] = (y * lax.rsqrt(ms + EPS) * gng_ref[...] * gconv).astype(yc_ref.dtype)


def _proj_conv(x2, ln_g, w_in, b_in, dw_w, dw_b, cln_g, cln_b, pw_w, pw_b, gn_g, batch, seq):
    tiles = seq // PROJ_TILE
    halo_per_tile = PROJ_TILE // HALO
    n_halo_blocks = batch * seq // HALO

    def main_map(b, t):
        return (b * tiles + t, 0)

    def prev_map(b, t):
        return (jnp.maximum((b * tiles + t) * halo_per_tile - 1, 0), 0)

    def next_map(b, t):
        return (jnp.minimum((b * tiles + t + 1) * halo_per_tile, n_halo_blocks - 1), 0)

    const = lambda b, t: (0, 0)
    out = jax.ShapeDtypeStruct((batch * seq, CONV_CH), jnp.bfloat16)
    tile = pl.BlockSpec((PROJ_TILE, CONV_CH), main_map)
    return pl.pallas_call(
        _proj_conv_kernel,
        grid=(batch, tiles),
        in_specs=[
            pl.BlockSpec((HALO, D_MODEL), prev_map),
            pl.BlockSpec((PROJ_TILE, D_MODEL), main_map),
            pl.BlockSpec((HALO, D_MODEL), next_map),
            pl.BlockSpec((1, D_MODEL), const),
            pl.BlockSpec((D_MODEL, PROJ_OUT), const),
            pl.BlockSpec((1, PROJ_OUT), const),
            pl.BlockSpec((CONV_WIDTH, CONV_CH), const),
            pl.BlockSpec((1, CONV_CH), const),
            pl.BlockSpec((1, CONV_CH), const),
            pl.BlockSpec((1, CONV_CH), const),
            pl.BlockSpec((CONV_CH, CONV_CH), const),
            pl.BlockSpec((1, CONV_CH), const),
            pl.BlockSpec((1, CONV_CH), const),
        ],
        out_specs=[tile] * 5,
        out_shape=[out] * 5,
        scratch_shapes=[
            pltpu.VMEM((PROJ_TILE + 2 * HALO, D_MODEL), jnp.bfloat16),
            pltpu.VMEM((N_LANE_CHUNKS, CONV_U_ROWS, LANES), jnp.float32),
            pltpu.VMEM((N_LANE_CHUNKS, CONV_C_ROWS, LANES), jnp.float32),
        ],
        compiler_params=pltpu.CompilerParams(
            dimension_semantics=("arbitrary", "arbitrary"), vmem_limit_bytes=VMEM_LIMIT_BYTES),
        name="proj_conv",
    )(x2, x2, x2, ln_g, w_in, b_in, dw_w, dw_b, cln_g, cln_b, pw_w, pw_b, gn_g)


def _attn_out_kernel(q_ref, kp_ref, km_ref, kn_ref, vp_ref, vm_ref, vn_ref, ga_ref, yc_ref,
                     x_ref, rel_ref, gng_ref, wo_ref, fg_ref, o_ref,
                     kbuf_ref, vbuf_ref, ya_ref, bias_ref, *, n_rows):
    t = pl.program_id(1)

    @pl.when((pl.program_id(0) == 0) & (t == 0))
    def _build_bias():
        w = lax.broadcasted_iota(jnp.int32, (GRID_W, 2 * GRID_W), 0)
        c = lax.broadcasted_iota(jnp.int32, (GRID_W, 2 * GRID_W), 1) % GRID_W
        start = jnp.clip(w - KW // 2, 0, GRID_W - KW)
        valid = (c >= start) & (c < start + KW)
        for h in range(N_HEADS):
            for a in range(BIAS_PAIRS):
                rows = jnp.broadcast_to(rel_ref[h, a:a + 1, :], (GRID_W, 2 * GRID_W))
                toep = pltpu.roll(rows, 0, 1, stride=1, stride_axis=0)
                bias_ref[h, a] = jnp.where(valid, toep * LOG2_E, MASK_VALUE)

    kbuf_ref[0:KV_HALO, :] = kp_ref[...]
    kbuf_ref[KV_HALO:KV_HALO + TOKEN_TILE, :] = km_ref[...]
    kbuf_ref[KV_HALO + TOKEN_TILE:, :] = kn_ref[...]
    vbuf_ref[0:KV_HALO, :] = vp_ref[...]
    vbuf_ref[KV_HALO:KV_HALO + TOKEN_TILE, :] = vm_ref[...]
    vbuf_ref[KV_HALO + TOKEN_TILE:, :] = vn_ref[...]

    lane_head = lax.broadcasted_iota(jnp.int32, (GRID_W, GROUP_CH), 1) // HEAD_DIM
    row0 = t * ROWS_PER_TILE
    for i in range(ROWS_PER_TILE):
        r = row0 + i
        rs = jnp.clip(r - MAX_KH // 2, 0, n_rows - MAX_KH)
        off = pl.multiple_of((rs - row0 + KV_HALO_ROWS) * GRID_W, GRID_W)
        cls = r - rs
        q_row = q_ref[i * GRID_W:(i + 1) * GRID_W, :]
        outs = []
        for gi in range(N_HEAD_GROUPS):
            ch = slice(gi * GROUP_CH, (gi + 1) * GROUP_CH)
            qg = q_row[:, ch]
            qbd = jnp.concatenate(
                [jnp.where(lane_head == h, qg, jnp.zeros_like(qg)) for h in range(HEADS_PER_GROUP)],
                axis=0)
            k_win = kbuf_ref[pl.ds(off, WIN_KEYS), ch]
            v_win = vbuf_ref[pl.ds(off, WIN_KEYS), ch]
            s = lax.dot_general(qbd, k_win, (((1,), (1,)), ((), ())),
                                preferred_element_type=jnp.float32)
            s = s + jnp.concatenate(
                [jnp.concatenate([bias_ref[gi * HEADS_PER_GROUP + h, 2 * p - cls + MAX_KH - 1]
                                  for p in range(MAX_KH // 2)], axis=1)
                 for h in range(HEADS_PER_GROUP)], axis=0)
            m = jnp.max(s, axis=-1, keepdims=True)
            e = jnp.exp2(s - m)
            l = jnp.sum(e, axis=-1, keepdims=True)
            pv = jnp.dot(e.astype(jnp.bfloat16), v_win, preferred_element_type=jnp.float32)
            pv = pv / l
            o = jnp.zeros((GRID_W, GROUP_CH), jnp.float32)
            for h in range(HEADS_PER_GROUP):
                o = o + jnp.where(lane_head == h, pv[h * GRID_W:(h + 1) * GRID_W, :], 0.0)
            outs.append(o)
        ya_ref[i * GRID_W:(i + 1) * GRID_W, :] = jnp.concatenate(outs, axis=-1)

    ya = ya_ref[...]
    ms = _group_mean_square(ya, N_HEADS)
    yan = ya * lax.rsqrt(ms + EPS) * gng_ref[...] * ga_ref[...].astype(jnp.float32)
    h = (x_ref[...]
         + jnp.dot(yc_ref[...], wo_ref[0:CONV_CH, :], preferred_element_type=jnp.float32)
         + jnp.dot(yan.astype(jnp.bfloat16), wo_ref[CONV_CH:, :],
                   preferred_element_type=jnp.float32))
    hms = jnp.mean(h * h, axis=-1, keepdims=True)
    o_ref[...] = (h * lax.rsqrt(hms + EPS) * fg_ref[...]).astype(o_ref.dtype)


def _attn_out(q, k, v, gatt, yc, x2, rel_rows, gn_g, w_out, final_g, batch, seq):
    tiles = seq // TOKEN_TILE
    n_rows = seq // GRID_W
    halo_per_tile = TOKEN_TILE // KV_HALO
    n_halo_blocks = batch * seq // KV_HALO

    def main_map(b, t):
        return (b * tiles + t, 0)

    def prev_map(b, t):
        return (jnp.maximum((b * tiles + t) * halo_per_tile - 1, 0), 0)

    def next_map(b, t):
        return (jnp.minimum((b * tiles + t + 1) * halo_per_tile, n_halo_blocks - 1), 0)

    const2 = lambda b, t: (0, 0)
    tile = pl.BlockSpec((TOKEN_TILE, ATT_CH), main_map)
    halo_p = pl.BlockSpec((KV_HALO, ATT_CH), prev_map)
    halo_n = pl.BlockSpec((KV_HALO, ATT_CH), next_map)
    return pl.pallas_call(
        functools.partial(_attn_out_kernel, n_rows=n_rows),
        grid=(batch, tiles),
        in_specs=[
            tile, halo_p, tile, halo_n, halo_p, tile, halo_n, tile, tile,
            pl.BlockSpec((TOKEN_TILE, D_MODEL), main_map),
            pl.BlockSpec(rel_rows.shape, lambda b, t: (0, 0, 0)),
            pl.BlockSpec((1, ATT_CH), const2),
            pl.BlockSpec((D_MODEL, D_MODEL), const2),
            pl.BlockSpec((1, D_MODEL), const2),
        ],
        out_specs=pl.BlockSpec((TOKEN_TILE, D_MODEL), main_map),
        out_shape=jax.ShapeDtypeStruct((batch * seq, D_MODEL), x2.dtype),
        scratch_shapes=[
            pltpu.VMEM((TOKEN_TILE + 2 * KV_HALO, ATT_CH), jnp.bfloat16),
            pltpu.VMEM((TOKEN_TILE + 2 * KV_HALO, ATT_CH), jnp.bfloat16),
            pltpu.VMEM((TOKEN_TILE, ATT_CH), jnp.float32),
            pltpu.VMEM((N_HEADS, BIAS_PAIRS, GRID_W, 2 * GRID_W), jnp.float32),
        ],
        compiler_params=pltpu.CompilerParams(
            dimension_semantics=("arbitrary", "arbitrary"), vmem_limit_bytes=VMEM_LIMIT_BYTES),
        name="attn_out",
    )(q, k, k, k, v, v, v, gatt, yc, x2, rel_rows, gn_g, w_out, final_g)


def _paired_rel_rows(rpb):
    n_heads, n_row_rel, n_col_rel = rpb.shape
    gap = jnp.zeros((n_heads, BIAS_PAIRS, GRID_W - n_col_rel), rpb.dtype)
    first, second = rpb[:, :BIAS_PAIRS], rpb[:, 1:]
    return jnp.concatenate(
        [first[..., KW - 1:], gap, second, gap, first[..., :KW - 1]], axis=-1).astype(jnp.float32)


def kernel(x, ln_g, w_in, b_in, dw_w, dw_b, cln_g, cln_b, pw_w, pw_b, rpb,
           gn_conv_g, gn_att_g, w_out, final_g):
    batch, seq, d_model = x.shape
    depth = w_in.shape[0]
    assert (d_model == D_MODEL and seq % PROJ_TILE == 0 and seq % TOKEN_TILE == 0
            and seq // GRID_W >= MAX_KH)
    bf16 = jnp.bfloat16
    h = x.reshape(batch * seq, d_model)
    for l in range(depth):
        last = l == depth - 1
        yc, q, k, v, gatt = _proj_conv(
            h, ln_g[l][None], w_in[l].astype(bf16), b_in[l][None], dw_w[l], dw_b[l][None],
            cln_g[l][None], cln_b[l][None], pw_w[l].astype(bf16), pw_b[l][None],
            gn_conv_g[l][None], batch, seq)
        assert last, "multi-layer stacks need an un-normalised attn_out variant"
        h = _attn_out(q, k, v, gatt, yc, h, _paired_rel_rows(rpb[l]), gn_att_g[l][None],
                      w_out[l].astype(bf16), final_g[None], batch, seq)
    return h.reshape(batch, seq, d_model)
```

```python
import functools

import jax
import jax.numpy as jnp
from jax import lax
from jax.experimental import pallas as pl
from jax.experimental.pallas import tpu as pltpu

D_MODEL = 1024
CONV_CH = 512
CONV_GROUPS = 8
N_HEADS = 8
HEAD_DIM = 64
ATT_CH = N_HEADS * HEAD_DIM
CONV_WIDTH = 31
CONV_PAD = CONV_WIDTH // 2
GRID_W = 64
MAX_KH = 8
KW = 16
EPS = 1e-6
PROJ_OUT = 3 * CONV_CH + 4 * ATT_CH

SUBLANES = 8
LANES = 128
HALO = 16
PROJ_TILE = 1024
TOKEN_TILE = 512
N_LANE_CHUNKS = CONV_CH // LANES
BF16_ROWS = 16
CONV_SEG = PROJ_TILE // BF16_ROWS + 2
CONV_C_ROWS = BF16_ROWS * CONV_SEG
CONV_U_ROWS = CONV_C_ROWS + 2 * HALO
CONV_TILES = CONV_SEG + 2 * HALO
CONV_BF16_TAPS = 8
ROWS_PER_TILE = TOKEN_TILE // GRID_W
KV_HALO_ROWS = MAX_KH // 2
KV_HALO = KV_HALO_ROWS * GRID_W
HEADS_PER_GROUP = 4
GROUP_CH = HEADS_PER_GROUP * HEAD_DIM
N_HEAD_GROUPS = N_HEADS // HEADS_PER_GROUP
WIN_KEYS = MAX_KH * GRID_W
MASK_VALUE = -1e30
LOG2_E = 1.4426950408889634
BIAS_PAIRS = 2 * MAX_KH - 2
VMEM_LIMIT_BYTES = 56 * 1024 * 1024


def _silu(z):
    return z * jax.nn.sigmoid(z)


def _group_mean_square(y, n_groups):
    c = y.shape[-1]
    gsz = c // n_groups
    gi = lax.broadcasted_iota(jnp.int32, (c, c), 0) // gsz
    gj = lax.broadcasted_iota(jnp.int32, (c, c), 1) // gsz
    same = (gi == gj).astype(jnp.bfloat16)
    ss = jnp.dot((y * y).astype(jnp.bfloat16), same, preferred_element_type=jnp.float32)
    return ss * (1.0 / gsz)


def _depthwise_conv(u_ref, u16_ref, c_ref, dww_ref):
    base = HALO - CONV_PAD
    half = CONV_SEG * SUBLANES

    def rows(k):
        return pl.ds(k, SUBLANES, stride=CONV_SEG), pl.ds(k + half, SUBLANES, stride=CONV_SEG)

    for ch in range(N_LANE_CHUNKS):
        lanes = slice(ch * LANES, (ch + 1) * LANES)
        for k in range(base, CONV_SEG + base + CONV_WIDTH - 1):
            lo, hi = rows(k)
            tile = jnp.concatenate([u_ref[ch, lo, :], u_ref[ch, hi, :]], axis=0)
            u16_ref[ch, BF16_ROWS * k:BF16_ROWS * (k + 1), :] = tile.astype(jnp.bfloat16)
        w = [jnp.broadcast_to(dww_ref[j:j + 1, lanes].astype(jnp.bfloat16), (BF16_ROWS, LANES))
             for j in range(CONV_WIDTH)]
        for k in range(CONV_SEG):
            acc = None
            for j_lo in range(0, CONV_WIDTH, CONV_BF16_TAPS):
                part = None
                for j in range(j_lo, min(j_lo + CONV_BF16_TAPS, CONV_WIDTH)):
                    r0 = BF16_ROWS * (k + j + base)
                    term = u16_ref[ch, r0:r0 + BF16_ROWS, :] * w[j]
                    part = term if part is None else part + term
                part = part.astype(jnp.float32)
                acc = part if acc is None else acc + part
            lo, hi = rows(k)
            c_ref[ch, lo, :] = acc[0:SUBLANES]
            c_ref[ch, hi, :] = acc[SUBLANES:]


def _proj_conv_kernel(xp_ref, xm_ref, xn_ref, lng_ref, w_ref, b_ref, dww_ref, dwb_ref, clng_ref,
                      clnb_ref, pww_ref, pwb_ref, gng_ref,
                      yc_ref, q_ref, k_ref, v_ref, ga_ref, hn_ref, u_ref, u16_ref, c_ref):
    t = pl.program_id(1)
    n_t = pl.num_programs(1)
    ext = PROJ_TILE + 2 * HALO

    def rms(x):
        ms = jnp.mean(x * x, axis=-1, keepdims=True)
        return (x * lax.rsqrt(ms + EPS) * lng_ref[...]).astype(hn_ref.dtype)

    hn_ref[0:HALO, :] = rms(xp_ref[...])
    hn_ref[HALO:HALO + PROJ_TILE, :] = rms(xm_ref[...])
    hn_ref[HALO + PROJ_TILE:, :] = rms(xn_ref[...])

    def proj(lhs, g):
        cols = slice(g * CONV_CH, (g + 1) * CONV_CH)
        return jnp.dot(lhs, w_ref[:, cols], preferred_element_type=jnp.float32) + b_ref[:, cols]

    hn_ext = hn_ref[...]
    u = proj(hn_ext, 0) * jax.nn.sigmoid(proj(hn_ext, 1))
    row = lax.broadcasted_iota(jnp.int32, (ext, 1), 0)
    in_seq = ((row >= HALO) | (t > 0)) & ((row < HALO + PROJ_TILE) | (t < n_t - 1))
    u = jnp.where(in_seq, u, 0.0)
    for ch in range(N_LANE_CHUNKS):
        u_ref[ch, 0:ext, :] = u[:, ch * LANES:(ch + 1) * LANES]
        u_ref[ch, ext:, :] = jnp.zeros((CONV_U_ROWS - ext, LANES), jnp.float32)

    hn = hn_ref[HALO:HALO + PROJ_TILE, :]
    gconv = _silu(proj(hn, 2))
    q_ref[...] = (proj(hn, 3) * (HEAD_DIM ** -0.5 * LOG2_E)).astype(q_ref.dtype)
    k_ref[...] = proj(hn, 4).astype(k_ref.dtype)
    v_ref[...] = proj(hn, 5).astype(v_ref.dtype)
    ga_ref[...] = _silu(proj(hn, 6)).astype(ga_ref.dtype)

    _depthwise_conv(u_ref, u16_ref, c_ref, dww_ref)
    c = jnp.concatenate([c_ref[ch, 0:PROJ_TILE, :] for ch in range(N_LANE_CHUNKS)],
                        axis=-1) + dwb_ref[...]
    mu = jnp.mean(c, axis=-1, keepdims=True)
    cc = c - mu
    var = jnp.mean(cc * cc, axis=-1, keepdims=True)
    n = cc * lax.rsqrt(var + EPS) * clng_ref[...] + clnb_ref[...]
    s = _silu(n).astype(jnp.bfloat16)
    y = jnp.dot(s, pww_ref[...], preferred_element_type=jnp.float32) + pwb_ref[...]
    ms = _group_mean_square(y, CONV_GROUPS)
    yc_ref[...] = (y * lax.rsqrt(ms + EPS) * gng_ref[...] * gconv).astype(yc_ref.dtype)


def _proj_conv(x2, ln_g, w_in, b_in, dw_w, dw_b, cln_g, cln_b, pw_w, pw_b, gn_g, batch, seq):
    tiles = seq // PROJ_TILE
    halo_per_tile = PROJ_TILE // HALO
    n_halo_blocks = batch * seq // HALO

    def main_map(b, t):
        return (b * tiles + t, 0)

    def prev_map(b, t):
        return (jnp.maximum((b * tiles + t) * halo_per_tile - 1, 0), 0)

    def next_map(b, t):
        return (jnp.minimum((b * tiles + t + 1) * halo_per_tile, n_halo_blocks - 1), 0)

    const = lambda b, t: (0, 0)
    out = jax.ShapeDtypeStruct((batch * seq, CONV_CH), jnp.bfloat16)
    tile = pl.BlockSpec((PROJ_TILE, CONV_CH), main_map)
    return pl.pallas_call(
        _proj_conv_kernel,
        grid=(batch, tiles),
        in_specs=[
            pl.BlockSpec((HALO, D_MODEL), prev_map),
            pl.BlockSpec((PROJ_TILE, D_MODEL), main_map),
            pl.BlockSpec((HALO, D_MODEL), next_map),
            pl.BlockSpec((1, D_MODEL), const),
            pl.BlockSpec((D_MODEL, PROJ_OUT), const),
            pl.BlockSpec((1, PROJ_OUT), const),
            pl.BlockSpec((CONV_WIDTH, CONV_CH), const),
            pl.BlockSpec((1, CONV_CH), const),
            pl.BlockSpec((1, CONV_CH), const),
            pl.BlockSpec((1, CONV_CH), const),
            pl.BlockSpec((CONV_CH, CONV_CH), const),
            pl.BlockSpec((1, CONV_CH), const),
            pl.BlockSpec((1, CONV_CH), const),
        ],
        out_specs=[tile] * 5,
        out_shape=[out] * 5,
        scratch_shapes=[
            pltpu.VMEM((PROJ_TILE + 2 * HALO, D_MODEL), jnp.bfloat16),
            pltpu.VMEM((N_LANE_CHUNKS, CONV_U_ROWS, LANES), jnp.float32),
            pltpu.VMEM((N_LANE_CHUNKS, BF16_ROWS * CONV_TILES, LANES), jnp.bfloat16),
            pltpu.VMEM((N_LANE_CHUNKS, CONV_C_ROWS, LANES), jnp.float32),
        ],
        compiler_params=pltpu.CompilerParams(
            dimension_semantics=("arbitrary", "arbitrary"), vmem_limit_bytes=VMEM_LIMIT_BYTES),
        name="proj_conv",
    )(x2, x2, x2, ln_g, w_in, b_in, dw_w, dw_b, cln_g, cln_b, pw_w, pw_b, gn_g)


def _attn_out_kernel(q_ref, kp_ref, km_ref, kn_ref, vp_ref, vm_ref, vn_ref, ga_ref, yc_ref,
                     x_ref, rel_ref, gng_ref, wo_ref, fg_ref, o_ref,
                     kbuf_ref, vbuf_ref, ya_ref, bias_ref, *, n_rows):
    t = pl.program_id(1)

    @pl.when((pl.program_id(0) == 0) & (t == 0))
    def _build_bias():
        w = lax.broadcasted_iota(jnp.int32, (GRID_W, 2 * GRID_W), 0)
        c = lax.broadcasted_iota(jnp.int32, (GRID_W, 2 * GRID_W), 1) % GRID_W
        start = jnp.clip(w - KW // 2, 0, GRID_W - KW)
        valid = (c >= start) & (c < start + KW)
        for h in range(N_HEADS):
            for a in range(BIAS_PAIRS):
                rows = jnp.broadcast_to(rel_ref[h, a:a + 1, :], (GRID_W, 2 * GRID_W))
                toep = pltpu.roll(rows, 0, 1, stride=1, stride_axis=0)
                bias_ref[h, a] = jnp.where(valid, toep * LOG2_E, MASK_VALUE)

    kbuf_ref[0:KV_HALO, :] = kp_ref[...]
    kbuf_ref[KV_HALO:KV_HALO + TOKEN_TILE, :] = km_ref[...]
    kbuf_ref[KV_HALO + TOKEN_TILE:, :] = kn_ref[...]
    vbuf_ref[0:KV_HALO, :] = vp_ref[...]
    vbuf_ref[KV_HALO:KV_HALO + TOKEN_TILE, :] = vm_ref[...]
    vbuf_ref[KV_HALO + TOKEN_TILE:, :] = vn_ref[...]

    lane_head = lax.broadcasted_iota(jnp.int32, (GRID_W, GROUP_CH), 1) // HEAD_DIM
    row0 = t * ROWS_PER_TILE
    for i in range(ROWS_PER_TILE):
        r = row0 + i
        rs = jnp.clip(r - MAX_KH // 2, 0, n_rows - MAX_KH)
        off = pl.multiple_of((rs - row0 + KV_HALO_ROWS) * GRID_W, GRID_W)
        cls = r - rs
        q_row = q_ref[i * GRID_W:(i + 1) * GRID_W, :]
        outs = []
        for gi in range(N_HEAD_GROUPS):
            ch = slice(gi * GROUP_CH, (gi + 1) * GROUP_CH)
            qg = q_row[:, ch]
            qbd = jnp.concatenate(
                [jnp.where(lane_head == h, qg, jnp.zeros_like(qg)) for h in range(HEADS_PER_GROUP)],
                axis=0)
            k_win = kbuf_ref[pl.ds(off, WIN_KEYS), ch]
            v_win = vbuf_ref[pl.ds(off, WIN_KEYS), ch]
            s = lax.dot_general(qbd, k_win, (((1,), (1,)), ((), ())),
                                preferred_element_type=jnp.float32)
            s = s + jnp.concatenate(
                [jnp.concatenate([bias_ref[gi * HEADS_PER_GROUP + h, 2 * p - cls + MAX_KH - 1]
                                  for p in range(MAX_KH // 2)], axis=1)
                 for h in range(HEADS_PER_GROUP)], axis=0)
            m = jnp.max(s, axis=-1, keepdims=True)
            e = jnp.exp2(s - m)
            l = jnp.sum(e, axis=-1, keepdims=True)
            pv = jnp.dot(e.astype(jnp.bfloat16), v_win, preferred_element_type=jnp.float32)
            pv = pv / l
            o = jnp.zeros((GRID_W, GROUP_CH), jnp.float32)
            for h in range(HEADS_PER_GROUP):
                o = o + jnp.where(lane_head == h, pv[h * GRID_W:(h + 1) * GRID_W, :], 0.0)
            outs.append(o)
        ya_ref[i * GRID_W:(i + 1) * GRID_W, :] = jnp.concatenate(outs, axis=-1)

    ya = ya_ref[...]
    ms = _group_mean_square(ya, N_HEADS)
    yan = ya * lax.rsqrt(ms + EPS) * gng_ref[...] * ga_ref[...].astype(jnp.float32)
    h = (x_ref[...]
         + jnp.dot(yc_ref[...], wo_ref[0:CONV_CH, :], preferred_element_type=jnp.float32)
         + jnp.dot(yan.astype(jnp.bfloat16), wo_ref[CONV_CH:, :],
                   preferred_element_type=jnp.float32))
    hms = jnp.mean(h * h, axis=-1, keepdims=True)
    o_ref[...] = (h * lax.rsqrt(hms + EPS) * fg_ref[...]).astype(o_ref.dtype)


def _attn_out(q, k, v, gatt, yc, x2, rel_rows, gn_g, w_out, final_g, batch, seq):
    tiles = seq // TOKEN_TILE
    n_rows = seq // GRID_W
    halo_per_tile = TOKEN_TILE // KV_HALO
    n_halo_blocks = batch * seq // KV_HALO

    def main_map(b, t):
        return (b * tiles + t, 0)

    def prev_map(b, t):
        return (jnp.maximum((b * tiles + t) * halo_per_tile - 1, 0), 0)

    def next_map(b, t):
        return (jnp.minimum((b * tiles + t + 1) * halo_per_tile, n_halo_blocks - 1), 0)

    const2 = lambda b, t: (0, 0)
    tile = pl.BlockSpec((TOKEN_TILE, ATT_CH), main_map)
    halo_p = pl.BlockSpec((KV_HALO, ATT_CH), prev_map)
    halo_n = pl.BlockSpec((KV_HALO, ATT_CH), next_map)
    return pl.pallas_call(
        functools.partial(_attn_out_kernel, n_rows=n_rows),
        grid=(batch, tiles),
        in_specs=[
            tile, halo_p, tile, halo_n, halo_p, tile, halo_n, tile, tile,
            pl.BlockSpec((TOKEN_TILE, D_MODEL), main_map),
            pl.BlockSpec(rel_rows.shape, lambda b, t: (0, 0, 0)),
            pl.BlockSpec((1, ATT_CH), const2),
            pl.BlockSpec((D_MODEL, D_MODEL), const2),
            pl.BlockSpec((1, D_MODEL), const2),
        ],
        out_specs=pl.BlockSpec((TOKEN_TILE, D_MODEL), main_map),
        out_shape=jax.ShapeDtypeStruct((batch * seq, D_MODEL), x2.dtype),
        scratch_shapes=[
            pltpu.VMEM((TOKEN_TILE + 2 * KV_HALO, ATT_CH), jnp.bfloat16),
            pltpu.VMEM((TOKEN_TILE + 2 * KV_HALO, ATT_CH), jnp.bfloat16),
            pltpu.VMEM((TOKEN_TILE, ATT_CH), jnp.float32),
            pltpu.VMEM((N_HEADS, BIAS_PAIRS, GRID_W, 2 * GRID_W), jnp.float32),
        ],
        compiler_params=pltpu.CompilerParams(
            dimension_semantics=("arbitrary", "arbitrary"), vmem_limit_bytes=VMEM_LIMIT_BYTES),
        name="attn_out",
    )(q, k, k, k, v, v, v, gatt, yc, x2, rel_rows, gn_g, w_out, final_g)


def _paired_rel_rows(rpb):
    n_heads, n_row_rel, n_col_rel = rpb.shape
    gap = jnp.zeros((n_heads, BIAS_PAIRS, GRID_W - n_col_rel), rpb.dtype)
    first, second = rpb[:, :BIAS_PAIRS], rpb[:, 1:]
    return jnp.concatenate(
        [first[..., KW - 1:], gap, second, gap, first[..., :KW - 1]], axis=-1).astype(jnp.float32)


def kernel(x, ln_g, w_in, b_in, dw_w, dw_b, cln_g, cln_b, pw_w, pw_b, rpb,
           gn_conv_g, gn_att_g, w_out, final_g):
    batch, seq, d_model = x.shape
    depth = w_in.shape[0]
    assert (d_model == D_MODEL and seq % PROJ_TILE == 0 and seq % TOKEN_TILE == 0
            and seq // GRID_W >= MAX_KH)
    bf16 = jnp.bfloat16
    h = x.reshape(batch * seq, d_model)
    for l in range(depth):
        last = l == depth - 1
        yc, q, k, v, gatt = _proj_conv(
            h, ln_g[l][None], w_in[l].astype(bf16), b_in[l][None], dw_w[l], dw_b[l][None],
            cln_g[l][None], cln_b[l][None], pw_w[l].astype(bf16), pw_b[l][None],
            gn_conv_g[l][None], batch, seq)
        assert last, "multi-layer stacks need an un-normalised attn_out variant"
        h = _attn_out(q, k, v, gatt, yc, h, _paired_rel_rows(rpb[l]), gn_att_g[l][None],
                      w_out[l].astype(bf16), final_g[None], batch, seq)
    return h.reshape(batch, seq, d_model)
```

```python
import functools

import jax
import jax.numpy as jnp
from jax import lax
from jax.experimental import pallas as pl
from jax.experimental.pallas import tpu as pltpu

D_MODEL = 1024
CONV_CH = 512
CONV_GROUPS = 8
N_HEADS = 8
HEAD_DIM = 64
ATT_CH = N_HEADS * HEAD_DIM
CONV_WIDTH = 31
CONV_PAD = CONV_WIDTH // 2
GRID_W = 64
MAX_KH = 8
KW = 16
EPS = 1e-6
PROJ_OUT = 3 * CONV_CH + 4 * ATT_CH

SUBLANES = 8
LANES = 128
HALO = 16
PROJ_TILE = 1024
TOKEN_TILE = 1024
N_LANE_CHUNKS = CONV_CH // LANES
CONV_STRIDE = PROJ_TILE // SUBLANES + 1
CONV_C_ROWS = SUBLANES * CONV_STRIDE
CONV_U_ROWS = -(-(CONV_C_ROWS + 2 * HALO) // SUBLANES) * SUBLANES
ROWS_PER_TILE = TOKEN_TILE // GRID_W
KV_HALO_ROWS = MAX_KH // 2
KV_HALO = KV_HALO_ROWS * GRID_W
HEADS_PER_GROUP = 4
GROUP_CH = HEADS_PER_GROUP * HEAD_DIM
N_HEAD_GROUPS = N_HEADS // HEADS_PER_GROUP
WIN_KEYS = MAX_KH * GRID_W
MASK_VALUE = -1e30
LOG2_E = 1.4426950408889634
BIAS_PAIRS = 2 * MAX_KH - 2
VMEM_LIMIT_BYTES = 56 * 1024 * 1024


def _silu(z):
    return z * jax.nn.sigmoid(z)


def _group_mean_square(y, n_groups):
    c = y.shape[-1]
    gsz = c // n_groups
    gi = lax.broadcasted_iota(jnp.int32, (c, c), 0) // gsz
    gj = lax.broadcasted_iota(jnp.int32, (c, c), 1) // gsz
    same = (gi == gj).astype(jnp.bfloat16)
    ss = jnp.dot((y * y).astype(jnp.bfloat16), same, preferred_element_type=jnp.float32)
    return ss * (1.0 / gsz)


def _depthwise_conv(u_ref, c_ref, dww_ref):
    base = HALO - CONV_PAD
    for ch in range(N_LANE_CHUNKS):
        lanes = slice(ch * LANES, (ch + 1) * LANES)
        w = [jnp.broadcast_to(dww_ref[j:j + 1, lanes], (SUBLANES, LANES)) for j in range(CONV_WIDTH)]
        for t0 in range(CONV_STRIDE):
            acc = u_ref[ch, pl.ds(base + t0, SUBLANES, stride=CONV_STRIDE), :] * w[0]
            for j in range(1, CONV_WIDTH):
                acc = acc + u_ref[ch, pl.ds(base + t0 + j, SUBLANES, stride=CONV_STRIDE), :] * w[j]
            c_ref[ch, pl.ds(t0, SUBLANES, stride=CONV_STRIDE), :] = acc


def _proj_conv_kernel(xp_ref, xm_ref, xn_ref, lng_ref, w_ref, b_ref, dww_ref, dwb_ref, clng_ref,
                      clnb_ref, pww_ref, pwb_ref, gng_ref,
                      yc_ref, q_ref, k_ref, v_ref, ga_ref, hn_ref, u_ref, c_ref):
    t = pl.program_id(1)
    n_t = pl.num_programs(1)
    ext = PROJ_TILE + 2 * HALO

    def rms(x):
        ms = jnp.mean(x * x, axis=-1, keepdims=True)
        return (x * lax.rsqrt(ms + EPS) * lng_ref[...]).astype(hn_ref.dtype)

    hn_ref[0:HALO, :] = rms(xp_ref[...])
    hn_ref[HALO:HALO + PROJ_TILE, :] = rms(xm_ref[...])
    hn_ref[HALO + PROJ_TILE:, :] = rms(xn_ref[...])

    def proj(lhs, g):
        cols = slice(g * CONV_CH, (g + 1) * CONV_CH)
        return jnp.dot(lhs, w_ref[:, cols], preferred_element_type=jnp.float32) + b_ref[:, cols]

    hn_ext = hn_ref[...]
    u = proj(hn_ext, 0) * jax.nn.sigmoid(proj(hn_ext, 1))
    row = lax.broadcasted_iota(jnp.int32, (ext, 1), 0)
    in_seq = ((row >= HALO) | (t > 0)) & ((row < HALO + PROJ_TILE) | (t < n_t - 1))
    u = jnp.where(in_seq, u, 0.0)
    for ch in range(N_LANE_CHUNKS):
        u_ref[ch, 0:ext, :] = u[:, ch * LANES:(ch + 1) * LANES]
        u_ref[ch, ext:, :] = jnp.zeros((CONV_U_ROWS - ext, LANES), jnp.float32)

    hn = hn_ref[HALO:HALO + PROJ_TILE, :]
    gconv = _silu(proj(hn, 2))
    q_ref[...] = (proj(hn, 3) * (HEAD_DIM ** -0.5 * LOG2_E)).astype(q_ref.dtype)
    k_ref[...] = proj(hn, 4).astype(k_ref.dtype)
    v_ref[...] = proj(hn, 5).astype(v_ref.dtype)
    ga_ref[...] = _silu(proj(hn, 6)).astype(ga_ref.dtype)

    _depthwise_conv(u_ref, c_ref, dww_ref)
    c = jnp.concatenate([c_ref[ch, 0:PROJ_TILE, :] for ch in range(N_LANE_CHUNKS)],
                        axis=-1) + dwb_ref[...]
    mu = jnp.mean(c, axis=-1, keepdims=True)
    cc = c - mu
    var = jnp.mean(cc * cc, axis=-1, keepdims=True)
    n = cc * lax.rsqrt(var + EPS) * clng_ref[...] + clnb_ref[...]
    s = _silu(n).astype(jnp.bfloat16)
    y = jnp.dot(s, pww_ref[...], preferred_element_type=jnp.float32) + pwb_ref[...]
    ms = _group_mean_square(y, CONV_GROUPS)
    yc_ref[...] = (y * lax.rsqrt(ms + EPS) * gng_ref[...] * gconv).astype(yc_ref.dtype)


def _proj_conv(x2, ln_g, w_in, b_in, dw_w, dw_b, cln_g, cln_b, pw_w, pw_b, gn_g, batch, seq):
    tiles = seq // PROJ_TILE
    halo_per_tile = PROJ_TILE // HALO
    n_halo_blocks = batch * seq // HALO

    def main_map(b, t):
        return (b * tiles + t, 0)

    def prev_map(b, t):
        return (jnp.maximum((b * tiles + t) * halo_per_tile - 1, 0), 0)

    def next_map(b, t):
        return (jnp.minimum((b * tiles + t + 1) * halo_per_tile, n_halo_blocks - 1), 0)

    const = lambda b, t: (0, 0)
    out = jax.ShapeDtypeStruct((batch * seq, CONV_CH), jnp.bfloat16)
    tile = pl.BlockSpec((PROJ_TILE, CONV_CH), main_map)
    return pl.pallas_call(
        _proj_conv_kernel,
        grid=(batch, tiles),
        in_specs=[
            pl.BlockSpec((HALO, D_MODEL), prev_map),
            pl.BlockSpec((PROJ_TILE, D_MODEL), main_map),
            pl.BlockSpec((HALO, D_MODEL), next_map),
            pl.BlockSpec((1, D_MODEL), const),
            pl.BlockSpec((D_MODEL, PROJ_OUT), const),
            pl.BlockSpec((1, PROJ_OUT), const),
            pl.BlockSpec((CONV_WIDTH, CONV_CH), const),
            pl.BlockSpec((1, CONV_CH), const),
            pl.BlockSpec((1, CONV_CH), const),
            pl.BlockSpec((1, CONV_CH), const),
            pl.BlockSpec((CONV_CH, CONV_CH), const),
            pl.BlockSpec((1, CONV_CH), const),
            pl.BlockSpec((1, CONV_CH), const),
        ],
        out_specs=[tile] * 5,
        out_shape=[out] * 5,
        scratch_shapes=[
            pltpu.VMEM((PROJ_TILE + 2 * HALO, D_MODEL), jnp.bfloat16),
            pltpu.VMEM((N_LANE_CHUNKS, CONV_U_ROWS, LANES), jnp.float32),
            pltpu.VMEM((N_LANE_CHUNKS, CONV_C_ROWS, LANES), jnp.float32),
        ],
        compiler_params=pltpu.CompilerParams(
            dimension_semantics=("arbitrary", "arbitrary"), vmem_limit_bytes=VMEM_LIMIT_BYTES),
        name="proj_conv",
    )(x2, x2, x2, ln_g, w_in, b_in, dw_w, dw_b, cln_g, cln_b, pw_w, pw_b, gn_g)


def _attn_out_kernel(q_ref, kp_ref, km_ref, kn_ref, vp_ref, vm_ref, vn_ref, ga_ref, yc_ref,
                     x_ref, rel_ref, gng_ref, wo_ref, fg_ref, o_ref,
                     kbuf_ref, vbuf_ref, ya_ref, bias_ref, *, n_rows):
    t = pl.program_id(1)

    @pl.when((pl.program_id(0) == 0) & (t == 0))
    def _build_bias():
        w = lax.broadcasted_iota(jnp.int32, (GRID_W, 2 * GRID_W), 0)
        c = lax.broadcasted_iota(jnp.int32, (GRID_W, 2 * GRID_W), 1) % GRID_W
        start = jnp.clip(w - KW // 2, 0, GRID_W - KW)
        valid = (c >= start) & (c < start + KW)
        for h in range(N_HEADS):
            for a in range(BIAS_PAIRS):
                rows = jnp.broadcast_to(rel_ref[h, a:a + 1, :], (GRID_W, 2 * GRID_W))
                toep = pltpu.roll(rows, 0, 1, stride=1, stride_axis=0)
                bias_ref[h, a] = jnp.where(valid, toep * LOG2_E, MASK_VALUE)

    kbuf_ref[0:KV_HALO, :] = kp_ref[...]
    kbuf_ref[KV_HALO:KV_HALO + TOKEN_TILE, :] = km_ref[...]
    kbuf_ref[KV_HALO + TOKEN_TILE:, :] = kn_ref[...]
    vbuf_ref[0:KV_HALO, :] = vp_ref[...]
    vbuf_ref[KV_HALO:KV_HALO + TOKEN_TILE, :] = vm_ref[...]
    vbuf_ref[KV_HALO + TOKEN_TILE:, :] = vn_ref[...]

    lane_head = lax.broadcasted_iota(jnp.int32, (GRID_W, GROUP_CH), 1) // HEAD_DIM
    row0 = t * ROWS_PER_TILE
    for i in range(ROWS_PER_TILE):
        r = row0 + i
        rs = jnp.clip(r - MAX_KH // 2, 0, n_rows - MAX_KH)
        off = pl.multiple_of((rs - row0 + KV_HALO_ROWS) * GRID_W, GRID_W)
        cls = r - rs
        q_row = q_ref[i * GRID_W:(i + 1) * GRID_W, :]
        outs = []
        for gi in range(N_HEAD_GROUPS):
            ch = slice(gi * GROUP_CH, (gi + 1) * GROUP_CH)
            qg = q_row[:, ch]
            qbd = jnp.concatenate(
                [jnp.where(lane_head == h, qg, jnp.zeros_like(qg)) for h in range(HEADS_PER_GROUP)],
                axis=0)
            k_win = kbuf_ref[pl.ds(off, WIN_KEYS), ch]
            v_win = vbuf_ref[pl.ds(off, WIN_KEYS), ch]
            s = lax.dot_general(qbd, k_win, (((1,), (1,)), ((), ())),
                                preferred_element_type=jnp.float32)
            s = s + jnp.concatenate(
                [jnp.concatenate([bias_ref[gi * HEADS_PER_GROUP + h, 2 * p - cls + MAX_KH - 1]
                                  for p in range(MAX_KH // 2)], axis=1)
                 for h in range(HEADS_PER_GROUP)], axis=0)
            m = jnp.max(s, axis=-1, keepdims=True)
            e = jnp.exp2(s - m)
            l = jnp.sum(e, axis=-1, keepdims=True)
            pv = jnp.dot(e.astype(jnp.bfloat16), v_win, preferred_element_type=jnp.float32)
            pv = pv / l
            o = jnp.zeros((GRID_W, GROUP_CH), jnp.float32)
            for h in range(HEADS_PER_GROUP):
                o = o + jnp.where(lane_head == h, pv[h * GRID_W:(h + 1) * GRID_W, :], 0.0)
            outs.append(o)
        ya_ref[i * GRID_W:(i + 1) * GRID_W, :] = jnp.concatenate(outs, axis=-1)

    ya = ya_ref[...]
    ms = _group_mean_square(ya, N_HEADS)
    yan = ya * lax.rsqrt(ms + EPS) * gng_ref[...] * ga_ref[...].astype(jnp.float32)
    h = (x_ref[...]
         + jnp.dot(yc_ref[...], wo_ref[0:CONV_CH, :], preferred_element_type=jnp.float32)
         + jnp.dot(yan.astype(jnp.bfloat16), wo_ref[CONV_CH:, :],
                   preferred_element_type=jnp.float32))
    hms = jnp.mean(h * h, axis=-1, keepdims=True)
    o_ref[...] = (h * lax.rsqrt(hms + EPS) * fg_ref[...]).astype(o_ref.dtype)


def _attn_out(q, k, v, gatt, yc, x2, rel_rows, gn_g, w_out, final_g, batch, seq):
    tiles = seq // TOKEN_TILE
    n_rows = seq // GRID_W
    halo_per_tile = TOKEN_TILE // KV_HALO
    n_halo_blocks = batch * seq // KV_HALO

    def main_map(b, t):
        return (b * tiles + t, 0)

    def prev_map(b, t):
        return (jnp.maximum((b * tiles + t) * halo_per_tile - 1, 0), 0)

    def next_map(b, t):
        return (jnp.minimum((b * tiles + t + 1) * halo_per_tile, n_halo_blocks - 1), 0)

    const2 = lambda b, t: (0, 0)
    tile = pl.BlockSpec((TOKEN_TILE, ATT_CH), main_map)
    halo_p = pl.BlockSpec((KV_HALO, ATT_CH), prev_map)
    halo_n = pl.BlockSpec((KV_HALO, ATT_CH), next_map)
    return pl.pallas_call(
        functools.partial(_attn_out_kernel, n_rows=n_rows),
        grid=(batch, tiles),
        in_specs=[
            tile, halo_p, tile, halo_n, halo_p, tile, halo_n, tile, tile,
            pl.BlockSpec((TOKEN_TILE, D_MODEL), main_map),
            pl.BlockSpec(rel_rows.shape, lambda b, t: (0, 0, 0)),
            pl.BlockSpec((1, ATT_CH), const2),
            pl.BlockSpec((D_MODEL, D_MODEL), const2),
            pl.BlockSpec((1, D_MODEL), const2),
        ],
        out_specs=pl.BlockSpec((TOKEN_TILE, D_MODEL), main_map),
        out_shape=jax.ShapeDtypeStruct((batch * seq, D_MODEL), x2.dtype),
        scratch_shapes=[
            pltpu.VMEM((TOKEN_TILE + 2 * KV_HALO, ATT_CH), jnp.bfloat16),
            pltpu.VMEM((TOKEN_TILE + 2 * KV_HALO, ATT_CH), jnp.bfloat16),
            pltpu.VMEM((TOKEN_TILE, ATT_CH), jnp.float32),
            pltpu.VMEM((N_HEADS, BIAS_PAIRS, GRID_W, 2 * GRID_W), jnp.float32),
        ],
        compiler_params=pltpu.CompilerParams(
            dimension_semantics=("arbitrary", "arbitrary"), vmem_limit_bytes=VMEM_LIMIT_BYTES),
        name="attn_out",
    )(q, k, k, k, v, v, v, gatt, yc, x2, rel_rows, gn_g, w_out, final_g)


def _paired_rel_rows(rpb):
    n_heads, n_row_rel, n_col_rel = rpb.shape
    gap = jnp.zeros((n_heads, BIAS_PAIRS, GRID_W - n_col_rel), rpb.dtype)
    first, second = rpb[:, :BIAS_PAIRS], rpb[:, 1:]
    return jnp.concatenate(
        [first[..., KW - 1:], gap, second, gap, first[..., :KW - 1]], axis=-1).astype(jnp.float32)


def kernel(x, ln_g, w_in, b_in, dw_w, dw_b, cln_g, cln_b, pw_w, pw_b, rpb,
           gn_conv_g, gn_att_g, w_out, final_g):
    batch, seq, d_model = x.shape
    depth = w_in.shape[0]
    assert (d_model == D_MODEL and seq % PROJ_TILE == 0 and seq % TOKEN_TILE == 0
            and seq // GRID_W >= MAX_KH)
    bf16 = jnp.bfloat16
    h = x.reshape(batch * seq, d_model)
    for l in range(depth):
        last = l == depth - 1
        yc, q, k, v, gatt = _proj_conv(
            h, ln_g[l][None], w_in[l].astype(bf16), b_in[l][None], dw_w[l], dw_b[l][None],
            cln_g[l][None], cln_b[l][None], pw_w[l].astype(bf16), pw_b[l][None],
            gn_conv_g[l][None], batch, seq)
        assert last, "multi-layer stacks need an un-normalised attn_out variant"
        h = _attn_out(q, k, v, gatt, yc, h, _paired_rel_rows(rpb[l]), gn_att_g[l][None],
                      w_out[l].astype(bf16), final_g[None], batch, seq)
    return h.reshape(batch, seq, d_model)
```
